```python
import jax, jax.numpy as jnp
from jax import lax
import numpy as np

D_MODEL = 1024
BATCH = 8
SEQ = 4096
DEPTH = 2
DEC_BATCH = 8
DEC_SEQ = 16
PAST_LEN = 1024

CHUNK = 64
GROUP_W = D_MODEL // 4
HEAD_DIM = 64
N_HEADS_G = GROUP_W // HEAD_DIM
POOL_WINDOWS = (2, 4, 8, 16)
POOL_GW = GROUP_W // len(POOL_WINDOWS)
POOL_STATE = max(POOL_WINDOWS) - 1
CONV_W = 3
RWKV_DECAY_RANK = 32
RWKV_A_RANK = 32
RWKV_GATE_RANK = 64
RWKV_IN = 3 * GROUP_W + RWKV_DECAY_RANK + RWKV_A_RANK + RWKV_GATE_RANK
RWKV_SPLITS = (GROUP_W, 2 * GROUP_W, 3 * GROUP_W, 3 * GROUP_W + RWKV_DECAY_RANK,
               3 * GROUP_W + RWKV_DECAY_RANK + RWKV_A_RANK)
POOL_IN = GROUP_W
CONV_IN = 3 * GROUP_W
SB_IN = 3 * GROUP_W
PROJ_IN = POOL_IN + RWKV_IN + CONV_IN + SB_IN
PROJ_SPLITS = (POOL_IN, POOL_IN + RWKV_IN, POOL_IN + RWKV_IN + CONV_IN)
D_FF = 256 * ((8 * D_MODEL // 3 + 255) // 256)
Q_BLOCK = 128
RMS_EPS = 1e-6
GN_EPS = 64e-5

kernel_name = 'hybrid_stream_encoder_step'


def rmsnorm(x, g):
    xf = x.astype(jnp.float32)
    y = xf * lax.rsqrt(jnp.mean(xf * xf, axis=-1, keepdims=True) + RMS_EPS)
    return (y * g.astype(jnp.float32)).astype(x.dtype)


def swiglu(x, w_gate, w_up, w_down):
    return (jax.nn.silu(x @ w_gate) * (x @ w_up)) @ w_down


def pool_mixer(u, prefix, pos, w_grp, scale):
    T = u.shape[1]
    full = jnp.concatenate([prefix.astype(u.dtype), u], axis=1)
    ff = full.astype(jnp.float32)
    cs = jnp.concatenate([jnp.zeros_like(ff[:, :1]), jnp.cumsum(ff, axis=1)], axis=1)
    end = cs[:, POOL_STATE + 1:]
    outs = []
    for gi, w in enumerate(POOL_WINDOWS):
        sl = slice(gi * POOL_GW, (gi + 1) * POOL_GW)
        start = cs[:, POOL_STATE + 1 - w: POOL_STATE + 1 - w + T, sl]
        cnt = jnp.minimum(w, pos + 1).astype(jnp.float32)[None, :, None]
        d = (end[..., sl] - start) / cnt - u[..., sl].astype(jnp.float32)
        outs.append(d @ w_grp[gi])
    y = jnp.concatenate(outs, axis=-1) * scale
    return y.astype(u.dtype), full[:, -POOL_STATE:]


def rwkv7_mixer(u, shift_prev, wkv_prev, mu, w0, w2, a0, a2, g2, k_k, k_a, r_k, ln_w, ln_b):
    B, T, _ = u.shape
    f32 = jnp.float32
    prev = jnp.concatenate([shift_prev[:, None, :].astype(u.dtype), u[:, :-1]], axis=1)
    xs = u + (prev - u) * mu
    r, k, v, dw, da, dg = jnp.split(xs, RWKV_SPLITS, axis=-1)
    w = -jax.nn.softplus(-(w0 + jnp.tanh(dw) @ w2).astype(f32)) - 0.5
    decay = jnp.exp(-jnp.exp(w))
    a = jax.nn.sigmoid((a0 + da @ a2).astype(f32))
    g = jax.nn.sigmoid(dg) @ g2
    heads = lambda t: t.reshape(B, T, N_HEADS_G, HEAD_DIM)
    kk = heads((k * k_k).astype(f32))
    kk = kk / jnp.maximum(jnp.sqrt(jnp.sum(kk * kk, axis=-1, keepdims=True)), 1e-12)
    kh = heads(k.astype(f32) * (1.0 + (a - 1.0) * k_a.astype(f32)))
    rh, vh, ah, wh = heads(r.astype(f32)), heads(v.astype(f32)), heads(a), heads(decay)

    def step(S, inp):
        r_t, w_t, k_t, v_t, kk_t, a_t = inp
        s_kk = jnp.einsum('bhvk,bhk->bhv', S, kk_t)
        S = (S * w_t[:, :, None, :] - s_kk[..., None] * (kk_t * a_t)[:, :, None, :]
             + v_t[..., None] * k_t[:, :, None, :])
        return S, jnp.einsum('bhvk,bhk->bhv', S, r_t)

    seq = tuple(jnp.moveaxis(t, 1, 0) for t in (rh, wh, kh, vh, kk, ah))
    S_T, ys = lax.scan(step, wkv_prev.astype(f32), seq)
    y = jnp.moveaxis(ys, 0, 1)
    mean = jnp.mean(y, axis=-1, keepdims=True)
    var = jnp.mean(jnp.square(y - mean), axis=-1, keepdims=True)
    y = ((y - mean) * lax.rsqrt(var + GN_EPS)).reshape(B, T, GROUP_W) * ln_w + ln_b
    bonus = jnp.sum(rh * kh * r_k.astype(f32), axis=-1, keepdims=True) * vh
    y = (y + bonus.reshape(B, T, GROUP_W)) * g
    return y.astype(u.dtype), u[:, -1], S_T


def conv_mixer(u, prefix, conv_w):
    T = u.shape[1]
    b, c, xin = jnp.split(u, 3, axis=-1)
    zf = jnp.concatenate([prefix.astype(u.dtype), c * xin], axis=1)
    y = zf[:, 0:T] * conv_w[0]
    for tap in range(1, CONV_W):
        y = y + zf[:, tap:tap + T] * conv_w[tap]
    return b * y, zf[:, -(CONV_W - 1):]


def sb_attend(q, k, v, q_pos, k_pos):
    z = jnp.einsum('bqhd,bshd->bhqs', q, k).astype(jnp.float32) * (HEAD_DIM ** -0.5)
    visible = k_pos[None, :] < q_pos[:, None]
    log_keep = jnp.where(visible, jax.nn.log_sigmoid(-z), 0.0)
    log_after = lax.cumsum(log_keep, axis=3, reverse=True) - log_keep
    attn = jnp.where(visible, jnp.exp(jax.nn.log_sigmoid(z) + log_after), 0.0)
    return jnp.einsum('bhqs,bshd->bqhd', attn, v.astype(jnp.float32))


def stick_breaking(q, k, v, q_pos, k_pos):
    B, T = q.shape[0], q.shape[1]
    if T <= Q_BLOCK:
        return sb_attend(q, k, v, q_pos, k_pos)
    nb = T // Q_BLOCK
    qb = jnp.moveaxis(q.reshape(B, nb, Q_BLOCK, N_HEADS_G, HEAD_DIM), 1, 0)
    pb = q_pos.reshape(nb, Q_BLOCK)
    ob = lax.map(lambda qp: sb_attend(qp[0], k, v, qp[1], k_pos), (qb, pb))
    return jnp.moveaxis(ob, 0, 1).reshape(B, T, N_HEADS_G, HEAD_DIM)


def trunk(x, k_past, v_past, wkv0, shift0, conv0, pool0, prm):
    B, T, _ = x.shape
    past = k_past.shape[2]
    q_pos = past + jnp.arange(T, dtype=jnp.int32)
    k_pos = jnp.arange(past + T, dtype=jnp.int32)
    ks, vs, wkvs, shifts, convs, pools = [], [], [], [], [], []
    for l in range(DEPTH):
        g = prm['norm_g'][l]
        h = rmsnorm(x, g[0])
        x = x + 0.5 * rmsnorm(swiglu(h, prm['ffn_w_gate'][l, 0], prm['ffn_w_up'][l, 0],
                                     prm['ffn_w_down'][l, 0]), g[1])
        h = rmsnorm(x, g[2])
        u = h @ prm['w_in'][l]
        u_pool, u_rwkv, u_conv, u_sb = jnp.split(u, PROJ_SPLITS, axis=-1)
        y_pool, pool_new = pool_mixer(u_pool, pool0[l], q_pos, prm['pool_w'][l], prm['pool_scale'][l])
        y_rwkv, shift_new, wkv_new = rwkv7_mixer(
            u_rwkv, shift0[l], wkv0[l], prm['rwkv_mu'][l], prm['rwkv_w0'][l], prm['rwkv_w2'][l],
            prm['rwkv_a0'][l], prm['rwkv_a2'][l], prm['rwkv_g2'][l], prm['rwkv_k_k'][l],
            prm['rwkv_k_a'][l], prm['rwkv_r_k'][l], prm['rwkv_ln_w'][l], prm['rwkv_ln_b'][l])
        y_conv, conv_new = conv_mixer(u_conv, conv0[l], prm['conv_w'][l])
        q, k, v = [t.reshape(B, T, N_HEADS_G, HEAD_DIM) for t in jnp.split(u_sb, 3, axis=-1)]
        k_all = jnp.concatenate([k_past[l].astype(k.dtype), k], axis=1)
        v_all = jnp.concatenate([v_past[l].astype(v.dtype), v], axis=1)
        y_sb = stick_breaking(q, k_all, v_all, q_pos, k_pos).reshape(B, T, GROUP_W).astype(x.dtype)
        y = jnp.concatenate([y_pool, y_rwkv, y_conv, y_sb], axis=-1) @ prm['w_out'][l]
        x = x + rmsnorm(y, g[3])
        h = rmsnorm(x, g[4])
        x = x + 0.5 * rmsnorm(swiglu(h, prm['ffn_w_gate'][l, 1], prm['ffn_w_up'][l, 1],
                                     prm['ffn_w_down'][l, 1]), g[5])
        ks.append(k); vs.append(v); wkvs.append(wkv_new); shifts.append(shift_new)
        convs.append(conv_new); pools.append(pool_new)
    return (x, jnp.stack(ks), jnp.stack(vs), jnp.stack(wkvs), jnp.stack(shifts),
            jnp.stack(convs), jnp.stack(pools))


def setup_inputs(seed: int = 0) -> dict:
    key = jax.random.key(seed)
    kk = jax.random.split(key, 32)
    L, D, G, H, N = DEPTH, D_MODEL, GROUP_W, N_HEADS_G, HEAD_DIM

    def nrm(i, shape, s):
        return jax.random.normal(kk[i], shape, jnp.float32) * s

    return {
        'x_prompt': nrm(0, (BATCH, SEQ, D), 1.0),
        'x_sample': nrm(1, (DEC_BATCH, DEC_SEQ, D), 1.0),
        'cache_sb_k': nrm(2, (L, DEC_BATCH, PAST_LEN, H, N), 1.0),
        'cache_sb_v': nrm(3, (L, DEC_BATCH, PAST_LEN, H, N), 1.0),
        'state_wkv': nrm(4, (L, DEC_BATCH, H, N, N), 0.5),
        'state_shift': nrm(5, (L, DEC_BATCH, RWKV_IN), 1.0),
        'state_conv': nrm(6, (L, DEC_BATCH, CONV_W - 1, G), 1.0),
        'state_pool': nrm(7, (L, DEC_BATCH, POOL_STATE, G), 1.0),
        'norm_g': 1.0 + nrm(8, (L, 6, D), 0.05),
        'ffn_w_gate': nrm(9, (L, 2, D, D_FF), D ** -0.5),
        'ffn_w_up': nrm(10, (L, 2, D, D_FF), D ** -0.5),
        'ffn_w_down': nrm(11, (L, 2, D_FF, D), D_FF ** -0.5),
        'w_in': nrm(12, (L, D, PROJ_IN), D ** -0.5),
        'w_out': nrm(13, (L, D, D), D ** -0.5),
        'pool_w': nrm(14, (L, len(POOL_WINDOWS), POOL_GW, POOL_GW), POOL_GW ** -0.5),
        'pool_scale': 1.0 + nrm(15, (L, G), 0.1),
        'rwkv_mu': jax.random.uniform(kk[16], (L, RWKV_IN), jnp.float32),
        'rwkv_w0': jax.random.uniform(kk[17], (L, G), jnp.float32, -6.0, -1.0),
        'rwkv_w2': nrm(18, (L, RWKV_DECAY_RANK, G), 0.1),
        'rwkv_a0': nrm(19, (L, G), 0.1),
        'rwkv_a2': nrm(20, (L, RWKV_A_RANK, G), 0.1),
        'rwkv_g2': nrm(21, (L, RWKV_GATE_RANK, G), RWKV_GATE_RANK ** -0.5),
        'rwkv_k_k': 1.0 + nrm(22, (L, G), 0.1),
        'rwkv_k_a': 1.0 + nrm(23, (L, G), 0.1),
        'rwkv_r_k': nrm(24, (L, H, N), 0.1),
        'rwkv_ln_w': 1.0 + nrm(25, (L, G), 0.05),
        'rwkv_ln_b': nrm(26, (L, G), 0.02),
        'conv_w': nrm(27, (L, CONV_W, G), CONV_W ** -0.5),
    }


def reference(x_prompt, x_sample, cache_sb_k, cache_sb_v, state_wkv, state_shift, state_conv, state_pool,
              norm_g, ffn_w_gate, ffn_w_up, ffn_w_down, w_in, w_out, pool_w, pool_scale,
              rwkv_mu, rwkv_w0, rwkv_w2, rwkv_a0, rwkv_a2, rwkv_g2, rwkv_k_k, rwkv_k_a, rwkv_r_k,
              rwkv_ln_w, rwkv_ln_b, conv_w):
    prm = dict(norm_g=norm_g, ffn_w_gate=ffn_w_gate, ffn_w_up=ffn_w_up, ffn_w_down=ffn_w_down,
               w_in=w_in, w_out=w_out, pool_w=pool_w, pool_scale=pool_scale, rwkv_mu=rwkv_mu,
               rwkv_w0=rwkv_w0, rwkv_w2=rwkv_w2, rwkv_a0=rwkv_a0, rwkv_a2=rwkv_a2, rwkv_g2=rwkv_g2,
               rwkv_k_k=rwkv_k_k, rwkv_k_a=rwkv_k_a, rwkv_r_k=rwkv_r_k, rwkv_ln_w=rwkv_ln_w,
               rwkv_ln_b=rwkv_ln_b, conv_w=conv_w)
    B = x_prompt.shape[0]
    dt = x_prompt.dtype
    zk = jnp.zeros((DEPTH, B, 0, N_HEADS_G, HEAD_DIM), dt)
    z_wkv = jnp.zeros((DEPTH, B, N_HEADS_G, HEAD_DIM, HEAD_DIM), jnp.float32)
    z_shift = jnp.zeros((DEPTH, B, RWKV_IN), dt)
    z_conv = jnp.zeros((DEPTH, B, CONV_W - 1, GROUP_W), dt)
    z_pool = jnp.zeros((DEPTH, B, POOL_STATE, GROUP_W), dt)
    y_prompt, p_sb_k, p_sb_v, p_wkv, p_shift, p_conv, p_pool = trunk(
        x_prompt, zk, zk, z_wkv, z_shift, z_conv, z_pool, prm)
    y_sample, s_sb_k, s_sb_v, s_wkv, s_shift, s_conv, s_pool = trunk(
        x_sample, cache_sb_k, cache_sb_v, state_wkv, state_shift, state_conv, state_pool, prm)
    return (y_prompt, y_sample, p_sb_k, p_sb_v, p_wkv, p_shift, p_conv, p_pool,
            s_sb_k, s_sb_v, s_wkv, s_shift, s_conv, s_pool)
```

```python
import functools

import jax
import jax.numpy as jnp
from jax import lax
from jax.experimental import pallas as pl
from jax.experimental.pallas import tpu as pltpu

F32 = jnp.float32
BF16 = jnp.bfloat16

D_MODEL = 1024
GROUP_W = 256
HEAD_DIM = 64
N_HEADS = 4
POOL_WINDOWS = (2, 4, 8, 16)
POOL_STATE = 15
CONV_W = 3
RWKV_IN = 896
RWKV_LOWRANK_W = 128
D_FF = 2816
RMS_EPS = 1e-6
GN_EPS = 64e-5

FF_CHUNK = 256
TOKEN_TILE = 512
TIME_TILE = 512
SCAN_CHUNK = 64
SCAN_BLOCK = 16
ATTN_TILE = 256
VMEM_LIMIT = 56 * 1024 * 1024

HIGHEST = lax.Precision.HIGHEST


def _cparams(sem):
    return pltpu.CompilerParams(dimension_semantics=sem, vmem_limit_bytes=VMEM_LIMIT)


def _rms(x, g):
    return x * lax.rsqrt(jnp.mean(x * x, axis=-1, keepdims=True) + RMS_EPS) * g


def _dot(a, b):
    return jnp.dot(a.astype(BF16), b.astype(BF16), preferred_element_type=F32)


def _dot_nt(a, b, precision=None):
    return lax.dot_general(a, b, (((1,), (1,)), ((), ())), precision=precision,
                           preferred_element_type=F32)


def _dot_tn(a, b, precision=None):
    return lax.dot_general(a, b, (((0,), (0,)), ((), ())), precision=precision,
                           preferred_element_type=F32)


def _dot_hi(a, b):
    return jnp.dot(a, b, precision=HIGHEST, preferred_element_type=F32)


def _dot_split(a, ones_b):
    hi = a.astype(BF16)
    lo = (a - hi.astype(F32)).astype(BF16)
    return (jnp.dot(hi, ones_b, preferred_element_type=F32)
            + jnp.dot(lo, ones_b, preferred_element_type=F32))


def _full(shape):
    return pl.BlockSpec(shape, lambda *_: (0,) * len(shape))


def _ffn_body(x_ref, gpre_ref, gpost_ref, wg_ref, wu_ref, wd_ref, o_ref, a_ref):
    x = x_ref[...]
    hb = _rms(x, gpre_ref[...]).astype(BF16)
    for j in range(D_FF // FF_CHUNK):
        sl = slice(j * FF_CHUNK, (j + 1) * FF_CHUNK)
        g = jnp.dot(hb, wg_ref[:, sl], preferred_element_type=F32)
        u = jnp.dot(hb, wu_ref[:, sl], preferred_element_type=F32)
        a_ref[:, sl] = (g * jax.nn.sigmoid(g) * u).astype(BF16)
    y = jnp.dot(a_ref[...], wd_ref[...], preferred_element_type=F32)
    o_ref[...] = x + 0.5 * _rms(y, gpost_ref[...])


def _ffn(x, g_pre, g_post, wg, wu, wd):
    n = x.shape[0]
    tm = min(TOKEN_TILE, n)
    row = pl.BlockSpec((tm, D_MODEL), lambda i: (i, 0))
    return pl.pallas_call(
        _ffn_body,
        grid=(n // tm,),
        in_specs=[row, _full((1, D_MODEL)), _full((1, D_MODEL)),
                  _full((D_MODEL, D_FF)), _full((D_MODEL, D_FF)), _full((D_FF, D_MODEL))],
        out_specs=row,
        out_shape=jax.ShapeDtypeStruct((n, D_MODEL), F32),
        scratch_shapes=[pltpu.VMEM((tm, D_FF), BF16)],
        compiler_params=_cparams(("arbitrary",)),
        name="ffn",
    )(x, g_pre, g_post, wg, wu, wd)


_PROJ_EDGES = (0, 256, 1152, 1920, 2176, 2432, 2688)


def _inproj_body(x_ref, g_ref, w_ref, *o_refs):
    hb = _rms(x_ref[...], g_ref[...]).astype(BF16)
    for o_ref, lo, hi in zip(o_refs, _PROJ_EDGES[:-1], _PROJ_EDGES[1:]):
        o_ref[...] = jnp.dot(hb, w_ref[:, lo:hi], preferred_element_type=F32)


def _inproj(x, g, w_in):
    n = x.shape[0]
    tm = min(TOKEN_TILE, n)
    widths = [hi - lo for lo, hi in zip(_PROJ_EDGES[:-1], _PROJ_EDGES[1:])]
    return pl.pallas_call(
        _inproj_body,
        grid=(n // tm,),
        in_specs=[pl.BlockSpec((tm, D_MODEL), lambda i: (i, 0)), _full((1, D_MODEL)),
                  _full((D_MODEL, _PROJ_EDGES[-1]))],
        out_specs=[pl.BlockSpec((tm, w), lambda i: (i, 0)) for w in widths],
        out_shape=[jax.ShapeDtypeStruct((n, w), F32) for w in widths],
        compiler_params=_cparams(("arbitrary",)),
        name="inproj",
    )(x, g, w_in)


def _outproj_body(x_ref, a_ref, b_ref, c_ref, d_ref, w_ref, g_ref, o_ref):
    y = _dot(a_ref[...], w_ref[0 * GROUP_W:1 * GROUP_W, :])
    y += _dot(b_ref[...], w_ref[1 * GROUP_W:2 * GROUP_W, :])
    y += _dot(c_ref[...], w_ref[2 * GROUP_W:3 * GROUP_W, :])
    y += _dot(d_ref[...], w_ref[3 * GROUP_W:4 * GROUP_W, :])
    o_ref[...] = x_ref[...] + _rms(y, g_ref[...])


def _outproj(x, ya, yb, yc, yd, w_out, g):
    n = x.shape[0]
    tm = min(TOKEN_TILE, n)
    row = pl.BlockSpec((tm, D_MODEL), lambda i: (i, 0))
    grp = pl.BlockSpec((tm, GROUP_W), lambda i: (i, 0))
    return pl.pallas_call(
        _outproj_body,
        grid=(n // tm,),
        in_specs=[row, grp, grp, grp, grp, _full((D_MODEL, D_MODEL)), _full((1, D_MODEL))],
        out_specs=row,
        out_shape=jax.ShapeDtypeStruct((n, D_MODEL), F32),
        compiler_params=_cparams(("arbitrary",)),
        name="outproj",
    )(x, ya, yb, yc, yd, w_out, g)


_POOL_PAD = 16
_ROW_PAD = 8


def _local_body(past, up_ref, uc_ref, pst_ref, cst_ref, wbd_ref, scale_ref, cw_ref,
                yp_ref, yc_ref, cso_ref, pprev_ref, cprev_ref):
    t = pl.program_id(1)
    tm = up_ref.shape[1]

    @pl.when(t == 0)
    def _():
        pprev_ref[...] = pst_ref[0]
        cprev_ref[...] = cst_ref[0]

    u = up_ref[0]
    ext = jnp.concatenate([pprev_ref[...], u], axis=0)
    s2 = ext + pltpu.roll(ext, 1, 0)
    s4 = s2 + pltpu.roll(s2, 2, 0)
    s8 = s4 + pltpu.roll(s4, 4, 0)
    s16 = s8 + pltpu.roll(s8, 8, 0)
    grp = lax.broadcasted_iota(jnp.int32, (tm, GROUP_W), 1) // (GROUP_W // len(POOL_WINDOWS))
    pos1 = past + t * tm + lax.broadcasted_iota(jnp.int32, (tm, GROUP_W), 0) + 1
    win = jnp.where(grp == 0, s2[_POOL_PAD:], jnp.where(grp == 1, s4[_POOL_PAD:],
                    jnp.where(grp == 2, s8[_POOL_PAD:], s16[_POOL_PAD:])))
    width = jnp.where(grp == 0, 2, jnp.where(grp == 1, 4, jnp.where(grp == 2, 8, 16)))
    cnt = jnp.minimum(width, pos1).astype(F32)
    d = win / cnt - u
    yp_ref[0] = _dot(d, wbd_ref[...]) * scale_ref[...]
    pprev_ref[...] = ext[tm:]

    uc = uc_ref[0]
    b = uc[:, 0:GROUP_W]
    z = uc[:, GROUP_W:2 * GROUP_W] * uc[:, 2 * GROUP_W:3 * GROUP_W]
    extz = jnp.concatenate([cprev_ref[...], z], axis=0)
    y = (pltpu.roll(extz, 2, 0)[_ROW_PAD:] * cw_ref[0:1, :]
         + pltpu.roll(extz, 1, 0)[_ROW_PAD:] * cw_ref[1:2, :]
         + z * cw_ref[2:3, :])
    yc_ref[0] = b * y
    cprev_ref[...] = extz[tm:]
    cso_ref[0] = extz[tm:][_ROW_PAD - (CONV_W - 1):]


def _local(u_pool, u_conv, pool_state, conv_state, w_bd, scale, conv_w, past):
    bsz, tlen, _ = u_pool.shape
    tm = min(TIME_TILE, tlen)
    pst = jnp.pad(pool_state, ((0, 0), (_POOL_PAD - POOL_STATE, 0), (0, 0)))
    cst = jnp.pad(conv_state, ((0, 0), (_ROW_PAD - (CONV_W - 1), 0), (0, 0)))
    return pl.pallas_call(
        functools.partial(_local_body, past),
        grid=(bsz, tlen // tm),
        in_specs=[pl.BlockSpec((1, tm, GROUP_W), lambda b, t: (b, t, 0)),
                  pl.BlockSpec((1, tm, 3 * GROUP_W), lambda b, t: (b, t, 0)),
                  pl.BlockSpec((1, _POOL_PAD, GROUP_W), lambda b, t: (b, 0, 0)),
                  pl.BlockSpec((1, _ROW_PAD, GROUP_W), lambda b, t: (b, 0, 0)),
                  _full((GROUP_W, GROUP_W)), _full((1, GROUP_W)), _full((CONV_W, GROUP_W))],
        out_specs=[pl.BlockSpec((1, tm, GROUP_W), lambda b, t: (b, t, 0)),
                   pl.BlockSpec((1, tm, GROUP_W), lambda b, t: (b, t, 0)),
                   pl.BlockSpec((1, CONV_W - 1, GROUP_W), lambda b, t: (b, 0, 0))],
        out_shape=[jax.ShapeDtypeStruct((bsz, tlen, GROUP_W), F32),
                   jax.ShapeDtypeStruct((bsz, tlen, GROUP_W), F32),
                   jax.ShapeDtypeStruct((bsz, CONV_W - 1, GROUP_W), F32)],
        scratch_shapes=[pltpu.VMEM((_POOL_PAD, GROUP_W), F32), pltpu.VMEM((_ROW_PAD, GROUP_W), F32)],
        compiler_params=_cparams(("arbitrary", "arbitrary")),
        name="local_mixers",
    )(u_pool, u_conv, pst, cst, w_bd, scale, conv_w)


def _softplus(x):
    return jnp.maximum(x, 0.0) + jnp.log1p(jnp.exp(-jnp.abs(x)))


def _prep_body(u_ref, sh_ref, mu_ref, w0_ref, w2_ref, a0_ref, a2_ref, g2_ref, kk_ref, ka_ref, hsum_ref,
               r_ref, lw_ref, k_ref, v_ref, kap_ref, b_ref, g_ref, prev_ref):
    t = pl.program_id(1)
    tm = u_ref.shape[1]

    @pl.when(t == 0)
    def _():
        prev_ref[...] = sh_ref[0]

    u = u_ref[0]
    ext = jnp.concatenate([prev_ref[...], u], axis=0)
    prev = pltpu.roll(ext, 1, 0)[_ROW_PAD:]
    prev_ref[...] = ext[tm:]
    xs = u + (prev - u) * mu_ref[...]
    r = xs[:, 0:GROUP_W]
    k = xs[:, GROUP_W:2 * GROUP_W]
    v = xs[:, 2 * GROUP_W:3 * GROUP_W]
    low = xs[:, 3 * GROUP_W:]
    w = -_softplus(-(w0_ref[...] + _dot(jnp.tanh(low), w2_ref[...]))) - 0.5
    a = jax.nn.sigmoid(a0_ref[...] + _dot(low, a2_ref[...]))
    kk = k * kk_ref[...]
    norm = jnp.sqrt(_dot_split(kk * kk, hsum_ref[...]))
    kap = kk / jnp.maximum(norm, 1e-12)
    r_ref[0] = r
    lw_ref[0] = -jnp.exp(w)
    k_ref[0] = k * (1.0 + (a - 1.0) * ka_ref[...])
    v_ref[0] = v
    kap_ref[0] = kap
    b_ref[0] = kap * a
    g_ref[0] = _dot(jax.nn.sigmoid(low), g2_ref[...])


def _rwkv_prep(u, shift_state, mu, w0, w2p, a0, a2p, g2p, k_k, k_a, hsum):
    bsz, tlen, _ = u.shape
    tm = min(TIME_TILE, tlen)
    sh = jnp.pad(shift_state[:, None, :], ((0, 0), (_ROW_PAD - 1, 0), (0, 0)))
    grp = pl.BlockSpec((1, tm, GROUP_W), lambda b, t: (b, t, 0))
    vec = _full((1, GROUP_W))
    low = _full((RWKV_LOWRANK_W, GROUP_W))
    return pl.pallas_call(
        _prep_body,
        grid=(bsz, tlen // tm),
        in_specs=[pl.BlockSpec((1, tm, RWKV_IN), lambda b, t: (b, t, 0)),
                  pl.BlockSpec((1, _ROW_PAD, RWKV_IN), lambda b, t: (b, 0, 0)),
                  _full((1, RWKV_IN)), vec, low, vec, low, low, vec, vec, _full((GROUP_W, GROUP_W))],
        out_specs=[grp] * 7,
        out_shape=[jax.ShapeDtypeStruct((bsz, tlen, GROUP_W), F32)] * 7,
        scratch_shapes=[pltpu.VMEM((_ROW_PAD, RWKV_IN), F32)],
        compiler_params=_cparams(("arbitrary", "arbitrary")),
        name="rwkv_prep",
    )(u, sh, mu, w0, w2p, a0, a2p, g2p, k_k, k_a, hsum)


def _unit_lower_inverse(a, c):
    row = lax.broadcasted_iota(jnp.int32, (c, c), 0)
    col = lax.broadcasted_iota(jnp.int32, (c, c), 1)
    eye = (row == col).astype(F32)

    def series(m, order):
        x = eye - m
        p = m
        n = 2
        while n < order:
            p = _dot_hi(p, p)
            x = x + _dot_hi(x, p)
            n *= 2
        return x

    if c == SCAN_BLOCK:
        return series(a, c)
    same_block = (row // SCAN_BLOCK) == (col // SCAN_BLOCK)
    a_diag = jnp.where(same_block, a, 0.0)
    t_diag = series(a_diag, SCAN_BLOCK)
    b = _dot_hi(t_diag, a - a_diag)
    return _dot_hi(series(b, c // SCAN_BLOCK), t_diag)


def _scan_body(r_ref, lw_ref, k_ref, v_ref, kap_ref, b_ref, g_ref, s0_ref, rk_ref, lnw_ref, lnb_ref,
               y_ref, so_ref, s_ref):
    t = pl.program_id(1)
    c = r_ref.shape[1]

    @pl.when(t == 0)
    def _():
        s_ref[...] = s0_ref[0]

    row = lax.broadcasted_iota(jnp.int32, (c, c), 0)
    col = lax.broadcasted_iota(jnp.int32, (c, c), 1)
    incl = row >= col
    strict = row > col
    tri = incl.astype(F32)

    outs = []
    for h in range(N_HEADS):
        sl = slice(h * HEAD_DIM, (h + 1) * HEAD_DIM)
        r, lw, k, v = r_ref[0, :, sl], lw_ref[0, :, sl], k_ref[0, :, sl], v_ref[0, :, sl]
        kap, b = kap_ref[0, :, sl], b_ref[0, :, sl]
        s = s_ref[h]

        cum = _dot_hi(tri, lw)
        cmid = cum[c // 2 - 1:c // 2, :]
        cend = cum[c - 1:c, :]
        e_fwd = jnp.exp(cum - cmid)
        e_bwd = jnp.exp(cmid - cum)
        e_end = jnp.exp(cend - cum)
        r_t = r * e_fwd
        kap_t = kap * jnp.exp(cum - lw - cmid)
        k_t = k * e_bwd
        b_t = b * e_bwd
        r_0 = r * jnp.exp(cum)
        kap_0 = kap * jnp.exp(cum - lw)

        a_kb = jnp.where(strict, _dot_nt(kap_t, b_t, HIGHEST), 0.0)
        a_kk = jnp.where(strict, _dot_nt(kap_t, k_t, HIGHEST), 0.0)
        a_rb = jnp.where(incl, _dot_nt(r_t, b_t, HIGHEST), 0.0)
        a_rk = jnp.where(incl, _dot_nt(r_t, k_t, HIGHEST), 0.0)
        t_inv = _unit_lower_inverse(a_kb, c)

        u = _dot_hi(t_inv, _dot_nt(kap_0, s, HIGHEST) + _dot_hi(a_kk, v))
        y = _dot_nt(r_0, s, HIGHEST) + _dot_hi(a_rk, v) - _dot_hi(a_rb, u)
        s_ref[h] = (s * jnp.exp(cend) + _dot_tn(v, k * e_end, HIGHEST)
                    - _dot_tn(u, b * e_end, HIGHEST))

        mean = jnp.mean(y, axis=-1, keepdims=True)
        yc = y - mean
        var = jnp.mean(yc * yc, axis=-1, keepdims=True)
        yn = yc * lax.rsqrt(var + GN_EPS) * lnw_ref[:, sl] + lnb_ref[:, sl]
        bonus = jnp.sum(r * k * rk_ref[:, sl], axis=-1, keepdims=True) * v
        outs.append((yn + bonus) * g_ref[0, :, sl])

    y_ref[0] = jnp.concatenate(outs, axis=-1)
    so_ref[0] = s_ref[...]


def _rwkv_scan(r, lw, k, v, kap, b, g, s0, r_k, ln_w, ln_b):
    bsz, tlen, _ = r.shape
    c = SCAN_CHUNK
    grp = pl.BlockSpec((1, c, GROUP_W), lambda b_, t: (b_, t, 0))
    st = pl.BlockSpec((1, N_HEADS, HEAD_DIM, HEAD_DIM), lambda b_, t: (b_, 0, 0, 0))
    vec = _full((1, GROUP_W))
    return pl.pallas_call(
        _scan_body,
        grid=(bsz, tlen // c),
        in_specs=[grp] * 7 + [st, vec, vec, vec],
        out_specs=[grp, st],
        out_shape=[jax.ShapeDtypeStruct((bsz, tlen, GROUP_W), F32),
                   jax.ShapeDtypeStruct((bsz, N_HEADS, HEAD_DIM, HEAD_DIM), F32)],
        scratch_shapes=[pltpu.VMEM((N_HEADS, HEAD_DIM, HEAD_DIM), F32)],
        compiler_params=_cparams(("arbitrary", "arbitrary")),
        name="rwkv_scan",
    )(r, lw, k, v, kap, b, g, s0, r_k, ln_w, ln_b)


def _attn_body(past, q_ref, k_ref, v_ref, o_ref):
    qi = pl.program_id(1)
    tq = q_ref.shape[1]
    tk = ATTN_TILE
    q0 = past + qi * tq
    diag = q0 // tk
    scale = HEAD_DIM ** -0.5

    qs = [(q_ref[0, :, h * HEAD_DIM:(h + 1) * HEAD_DIM] * scale).astype(BF16) for h in range(N_HEADS)]
    later = (lax.broadcasted_iota(jnp.int32, (tk, tk), 0)
             > lax.broadcasted_iota(jnp.int32, (tk, tk), 1)).astype(BF16)

    def tile(j, carry, masked):
        accs, runs = carry
        start = pl.multiple_of(j * tk, tk)
        kt = k_ref[0, pl.ds(start, tk), :].astype(BF16)
        vt = v_ref[0, pl.ds(start, tk), :].astype(BF16)
        if masked:
            qpos = q0 + lax.broadcasted_iota(jnp.int32, (tq, tk), 0)
            kpos = j * tk + lax.broadcasted_iota(jnp.int32, (tq, tk), 1)
            vis = kpos < qpos
        new_accs, new_runs = [], []
        for h in range(N_HEADS):
            sl = slice(h * HEAD_DIM, (h + 1) * HEAD_DIM)
            z = _dot_nt(qs[h], kt[:, sl])
            sp = jnp.log1p(jnp.exp(-jnp.abs(z)))
            log_take = jnp.minimum(z, 0.0) - sp
            log_keep = log_take - z
            if masked:
                log_keep = jnp.where(vis, log_keep, 0.0)
            after = _dot_split(log_keep, later) + runs[h]
            p = jnp.exp(log_take + after)
            if masked:
                p = jnp.where(vis, p, 0.0)
            new_accs.append(accs[h] + jnp.dot(p.astype(BF16), vt[:, sl], preferred_element_type=F32))
            new_runs.append(runs[h] + jnp.sum(log_keep, axis=-1, keepdims=True))
        return tuple(new_accs), tuple(new_runs)

    init = (tuple(jnp.zeros((tq, HEAD_DIM), F32) for _ in range(N_HEADS)),
            tuple(jnp.zeros((tq, 1), F32) for _ in range(N_HEADS)))
    carry = tile(diag, init, True)
    accs, _ = lax.fori_loop(0, diag, lambda i, cr: tile(diag - 1 - i, cr, False), carry)
    o_ref[0] = jnp.concatenate(accs, axis=-1)


def _sb_attn(q, k_all, v_all, past):
    bsz, tlen, _ = q.shape
    tq = min(ATTN_TILE, tlen)
    slen = k_all.shape[1]
    kv = pl.BlockSpec((1, slen, GROUP_W), lambda b, i: (b, 0, 0))
    return pl.pallas_call(
        functools.partial(_attn_body, past),
        grid=(bsz, tlen // tq),
        in_specs=[pl.BlockSpec((1, tq, GROUP_W), lambda b, i: (b, i, 0)), kv, kv],
        out_specs=pl.BlockSpec((1, tq, GROUP_W), lambda b, i: (b, i, 0)),
        out_shape=jax.ShapeDtypeStruct((bsz, tlen, GROUP_W), F32),
        compiler_params=_cparams(("arbitrary", "arbitrary")),
        name="sb_attn",
    )(q, k_all, v_all)


def _pad_time(x, mult):
    pad = (-x.shape[1]) % mult
    return x if pad == 0 else jnp.pad(x, ((0, 0), (0, pad), (0, 0)))


def _prep_params(prm):
    out = []
    depth = prm["w_in"].shape[0]
    head_of = jnp.arange(GROUP_W) // HEAD_DIM
    hsum = (head_of[:, None] == head_of[None, :]).astype(BF16)
    row2 = lambda v: v.reshape(1, -1)
    for l in range(depth):
        w_bd = jax.scipy.linalg.block_diag(*[prm["pool_w"][l, i] for i in range(len(POOL_WINDOWS))])
        lowpad = lambda w, lo: jnp.zeros((RWKV_LOWRANK_W, GROUP_W), F32).at[lo:lo + w.shape[0]].set(w)
        out.append(dict(
            g=[row2(prm["norm_g"][l, i]) for i in range(6)],
            wg=[prm["ffn_w_gate"][l, i].astype(BF16) for i in range(2)],
            wu=[prm["ffn_w_up"][l, i].astype(BF16) for i in range(2)],
            wd=[prm["ffn_w_down"][l, i].astype(BF16) for i in range(2)],
            w_in=prm["w_in"][l].astype(BF16),
            w_out=prm["w_out"][l].astype(BF16),
            w_bd=w_bd.astype(BF16),
            pool_scale=row2(prm["pool_scale"][l]),
            conv_w=prm["conv_w"][l],
            mu=row2(prm["rwkv_mu"][l]),
            w0=row2(prm["rwkv_w0"][l]),
            w2p=lowpad(prm["rwkv_w2"][l], 0).astype(BF16),
            a0=row2(prm["rwkv_a0"][l]),
            a2p=lowpad(prm["rwkv_a2"][l], 32).astype(BF16),
            g2p=lowpad(prm["rwkv_g2"][l], 64).astype(BF16),
            k_k=row2(prm["rwkv_k_k"][l]),
            k_a=row2(prm["rwkv_k_a"][l]),
            r_k=row2(prm["rwkv_r_k"][l]),
            ln_w=row2(prm["rwkv_ln_w"][l]),
            ln_b=row2(prm["rwkv_ln_b"][l]),
            hsum=hsum,
        ))
    return out


def _trunk(x, k_past, v_past, wkv0, shift0, conv0, pool0, layers, past):
    bsz, tlen, _ = x.shape
    n = bsz * tlen
    x = x.reshape(n, D_MODEL)
    ks, vs, wkvs, shifts, convs, pools = [], [], [], [], [], []
    for l, p in enumerate(layers):
        x = _ffn(x, p["g"][0], p["g"][1], p["wg"][0], p["wu"][0], p["wd"][0])
        u_pool, u_rwkv, u_conv, q, k, v = _inproj(x, p["g"][2], p["w_in"])
        seq = lambda a: a.reshape(bsz, tlen, -1)
        u_pool, u_rwkv, u_conv, q, k, v = map(seq, (u_pool, u_rwkv, u_conv, q, k, v))

        y_pool, y_conv, conv_new = _local(u_pool, u_conv, pool0[l], conv0[l], p["w_bd"],
                                          p["pool_scale"], p["conv_w"], past)

        prep = _rwkv_prep(u_rwkv, shift0[l], p["mu"], p["w0"], p["w2p"], p["a0"], p["a2p"], p["g2p"],
                          p["k_k"], p["k_a"], p["hsum"])
        prep = [_pad_time(a, SCAN_CHUNK) for a in prep]
        y_rwkv, wkv_new = _rwkv_scan(*prep, wkv0[l], p["r_k"], p["ln_w"], p["ln_b"])
        y_rwkv = y_rwkv[:, :tlen]

        if past:
            k_all = jnp.concatenate([k_past[l].reshape(bsz, past, GROUP_W), k], axis=1)
            v_all = jnp.concatenate([v_past[l].reshape(bsz, past, GROUP_W), v], axis=1)
        else:
            k_all, v_all = k, v
        y_sb = _sb_attn(q, _pad_time(k_all, ATTN_TILE), _pad_time(v_all, ATTN_TILE), past)

        flat = lambda a: a.reshape(n, GROUP_W)
        x = _outproj(x, flat(y_pool), flat(y_rwkv), flat(y_conv), flat(y_sb), p["w_out"], p["g"][3])
        x = _ffn(x, p["g"][4], p["g"][5], p["wg"][1], p["wu"][1], p["wd"][1])

        ks.append(k.reshape(bsz, tlen, N_HEADS, HEAD_DIM))
        vs.append(v.reshape(bsz, tlen, N_HEADS, HEAD_DIM))
        wkvs.append(wkv_new)
        shifts.append(u_rwkv[:, -1])
        convs.append(conv_new)
        pools.append(u_pool[:, -POOL_STATE:])
    return (x.reshape(bsz, tlen, D_MODEL), jnp.stack(ks), jnp.stack(vs), jnp.stack(wkvs),
            jnp.stack(shifts), jnp.stack(convs), jnp.stack(pools))


def kernel(x_prompt, x_sample, cache_sb_k, cache_sb_v, state_wkv, state_shift, state_conv, state_pool,
           norm_g, ffn_w_gate, ffn_w_up, ffn_w_down, w_in, w_out, pool_w, pool_scale,
           rwkv_mu, rwkv_w0, rwkv_w2, rwkv_a0, rwkv_a2, rwkv_g2, rwkv_k_k, rwkv_k_a, rwkv_r_k,
           rwkv_ln_w, rwkv_ln_b, conv_w):
    layers = _prep_params(dict(
        norm_g=norm_g, ffn_w_gate=ffn_w_gate, ffn_w_up=ffn_w_up, ffn_w_down=ffn_w_down, w_in=w_in,
        w_out=w_out, pool_w=pool_w, pool_scale=pool_scale, rwkv_mu=rwkv_mu, rwkv_w0=rwkv_w0,
        rwkv_w2=rwkv_w2, rwkv_a0=rwkv_a0, rwkv_a2=rwkv_a2, rwkv_g2=rwkv_g2, rwkv_k_k=rwkv_k_k,
        rwkv_k_a=rwkv_k_a, rwkv_r_k=rwkv_r_k, rwkv_ln_w=rwkv_ln_w, rwkv_ln_b=rwkv_ln_b, conv_w=conv_w))
    depth = w_in.shape[0]
    bsz = x_prompt.shape[0]
    z_wkv = jnp.zeros((depth, bsz, N_HEADS, HEAD_DIM, HEAD_DIM), F32)
    z_shift = jnp.zeros((depth, bsz, RWKV_IN), F32)
    z_conv = jnp.zeros((depth, bsz, CONV_W - 1, GROUP_W), F32)
    z_pool = jnp.zeros((depth, bsz, POOL_STATE, GROUP_W), F32)
    prompt = _trunk(x_prompt, None, None, z_wkv, z_shift, z_conv, z_pool, layers, 0)
    sample = _trunk(x_sample, cache_sb_k, cache_sb_v, state_wkv, state_shift, state_conv, state_pool,
                    layers, cache_sb_k.shape[2])
    return (prompt[0], sample[0]) + prompt[1:] + sample[1:]
```

```python
import functools

import jax
import jax.numpy as jnp
from jax import lax
from jax.experimental import pallas as pl
from jax.experimental.pallas import tpu as pltpu

F32 = jnp.float32
BF16 = jnp.bfloat16

D_MODEL = 1024
GROUP_W = 256
HEAD_DIM = 64
N_HEADS = 4
POOL_WINDOWS = (2, 4, 8, 16)
POOL_STATE = 15
CONV_W = 3
RWKV_IN = 896
RWKV_LOWRANK_W = 128
D_FF = 2816
RMS_EPS = 1e-6
GN_EPS = 64e-5
LOG2_E = 1.4426950408889634

FF_CHUNK = 256
TOKEN_TILE = 512
TIME_TILE = 512
SCAN_CHUNK = 64
SCAN_BLOCK = 16
SCAN_BATCH = 2
INV_PASSES = 3
STATE_PASSES = 3
ATTN_TILE = 256
VMEM_LIMIT = 56 * 1024 * 1024


def _cparams(sem):
    return pltpu.CompilerParams(dimension_semantics=sem, vmem_limit_bytes=VMEM_LIMIT)


def _rms(x, g):
    return x * lax.rsqrt(jnp.mean(x * x, axis=-1, keepdims=True) + RMS_EPS) * g


def _dot(a, b):
    return jnp.dot(a.astype(BF16), b.astype(BF16), preferred_element_type=F32)


def _dot_nt(a, b):
    return lax.dot_general(a, b, (((1,), (1,)), ((), ())), preferred_element_type=F32)


def _split2(a):
    hi = a.astype(BF16)
    return hi, (a - hi.astype(F32)).astype(BF16)


def _dot_split(a, ones_b):
    hi, lo = _split2(a)
    return (jnp.dot(hi, ones_b, preferred_element_type=F32)
            + jnp.dot(lo, ones_b, preferred_element_type=F32))


def _full(shape):
    return pl.BlockSpec(shape, lambda *_: (0,) * len(shape))


def _ffn_body(x_ref, gpre_ref, gpost_ref, wg_ref, wu_ref, wd_ref, o_ref, a_ref):
    x = x_ref[...]
    hb = _rms(x, gpre_ref[...]).astype(BF16)
    for j in range(D_FF // FF_CHUNK):
        sl = slice(j * FF_CHUNK, (j + 1) * FF_CHUNK)
        g = jnp.dot(hb, wg_ref[:, sl], preferred_element_type=F32)
        u = jnp.dot(hb, wu_ref[:, sl], preferred_element_type=F32)
        a_ref[:, sl] = (g * jax.nn.sigmoid(g) * u).astype(BF16)
    y = jnp.dot(a_ref[...], wd_ref[...], preferred_element_type=F32)
    o_ref[...] = x + 0.5 * _rms(y, gpost_ref[...])


def _ffn(x, g_pre, g_post, wg, wu, wd):
    n = x.shape[0]
    tm = min(TOKEN_TILE, n)
    row = pl.BlockSpec((tm, D_MODEL), lambda i: (i, 0))
    return pl.pallas_call(
        _ffn_body,
        grid=(n // tm,),
        in_specs=[row, _full((1, D_MODEL)), _full((1, D_MODEL)),
                  _full((D_MODEL, D_FF)), _full((D_MODEL, D_FF)), _full((D_FF, D_MODEL))],
        out_specs=row,
        out_shape=jax.ShapeDtypeStruct((n, D_MODEL), F32),
        scratch_shapes=[pltpu.VMEM((tm, D_FF), BF16)],
        compiler_params=_cparams(("arbitrary",)),
        name="ffn",
    )(x, g_pre, g_post, wg, wu, wd)


_PROJ_EDGES = (0, 256, 1152, 1920, 2176, 2432, 2688)


def _inproj_body(x_ref, g_ref, w_ref, *o_refs):
    hb = _rms(x_ref[...], g_ref[...]).astype(BF16)
    for o_ref, lo, hi in zip(o_refs, _PROJ_EDGES[:-1], _PROJ_EDGES[1:]):
        o_ref[...] = jnp.dot(hb, w_ref[:, lo:hi], preferred_element_type=F32)


def _inproj(x, g, w_in):
    n = x.shape[0]
    tm = min(TOKEN_TILE, n)
    widths = [hi - lo for lo, hi in zip(_PROJ_EDGES[:-1], _PROJ_EDGES[1:])]
    return pl.pallas_call(
        _inproj_body,
        grid=(n // tm,),
        in_specs=[pl.BlockSpec((tm, D_MODEL), lambda i: (i, 0)), _full((1, D_MODEL)),
                  _full((D_MODEL, _PROJ_EDGES[-1]))],
        out_specs=[pl.BlockSpec((tm, w), lambda i: (i, 0)) for w in widths],
        out_shape=[jax.ShapeDtypeStruct((n, w), F32) for w in widths],
        compiler_params=_cparams(("arbitrary",)),
        name="inproj",
    )(x, g, w_in)


def _outproj_body(x_ref, a_ref, b_ref, c_ref, d_ref, w_ref, g_ref, o_ref):
    y = _dot(a_ref[...], w_ref[0 * GROUP_W:1 * GROUP_W, :])
    y += _dot(b_ref[...], w_ref[1 * GROUP_W:2 * GROUP_W, :])
    y += _dot(c_ref[...], w_ref[2 * GROUP_W:3 * GROUP_W, :])
    y += _dot(d_ref[...], w_ref[3 * GROUP_W:4 * GROUP_W, :])
    o_ref[...] = x_ref[...] + _rms(y, g_ref[...])


def _outproj(x, ya, yb, yc, yd, w_out, g):
    n = x.shape[0]
    tm = min(TOKEN_TILE, n)
    row = pl.BlockSpec((tm, D_MODEL), lambda i: (i, 0))
    grp = pl.BlockSpec((tm, GROUP_W), lambda i: (i, 0))
    return pl.pallas_call(
        _outproj_body,
        grid=(n // tm,),
        in_specs=[row, grp, grp, grp, grp, _full((D_MODEL, D_MODEL)), _full((1, D_MODEL))],
        out_specs=row,
        out_shape=jax.ShapeDtypeStruct((n, D_MODEL), F32),
        compiler_params=_cparams(("arbitrary",)),
        name="outproj",
    )(x, ya, yb, yc, yd, w_out, g)


_POOL_PAD = 16
_ROW_PAD = 8


def _local_body(past, up_ref, uc_ref, pst_ref, cst_ref, wbd_ref, scale_ref, cw_ref,
                yp_ref, yc_ref, cso_ref, pprev_ref, cprev_ref):
    t = pl.program_id(1)
    tm = up_ref.shape[1]

    @pl.when(t == 0)
    def _():
        pprev_ref[...] = pst_ref[0]
        cprev_ref[...] = cst_ref[0]

    u = up_ref[0]
    ext = jnp.concatenate([pprev_ref[...], u], axis=0)
    s2 = ext + pltpu.roll(ext, 1, 0)
    s4 = s2 + pltpu.roll(s2, 2, 0)
    s8 = s4 + pltpu.roll(s4, 4, 0)
    s16 = s8 + pltpu.roll(s8, 8, 0)
    grp = lax.broadcasted_iota(jnp.int32, (tm, GROUP_W), 1) // (GROUP_W // len(POOL_WINDOWS))
    pos1 = past + t * tm + lax.broadcasted_iota(jnp.int32, (tm, GROUP_W), 0) + 1
    win = jnp.where(grp == 0, s2[_POOL_PAD:], jnp.where(grp == 1, s4[_POOL_PAD:],
                    jnp.where(grp == 2, s8[_POOL_PAD:], s16[_POOL_PAD:])))
    width = jnp.where(grp == 0, 2, jnp.where(grp == 1, 4, jnp.where(grp == 2, 8, 16)))
    cnt = jnp.minimum(width, pos1).astype(F32)
    d = win / cnt - u
    yp_ref[0] = _dot(d, wbd_ref[...]) * scale_ref[...]
    pprev_ref[...] = ext[tm:]

    uc = uc_ref[0]
    b = uc[:, 0:GROUP_W]
    z = uc[:, GROUP_W:2 * GROUP_W] * uc[:, 2 * GROUP_W:3 * GROUP_W]
    extz = jnp.concatenate([cprev_ref[...], z], axis=0)
    y = (pltpu.roll(extz, 2, 0)[_ROW_PAD:] * cw_ref[0:1, :]
         + pltpu.roll(extz, 1, 0)[_ROW_PAD:] * cw_ref[1:2, :]
         + z * cw_ref[2:3, :])
    yc_ref[0] = b * y
    cprev_ref[...] = extz[tm:]
    cso_ref[0] = extz[tm:][_ROW_PAD - (CONV_W - 1):]


def _local(u_pool, u_conv, pool_state, conv_state, w_bd, scale, conv_w, past):
    bsz, tlen, _ = u_pool.shape
    tm = min(TIME_TILE, tlen)
    pst = jnp.pad(pool_state, ((0, 0), (_POOL_PAD - POOL_STATE, 0), (0, 0)))
    cst = jnp.pad(conv_state, ((0, 0), (_ROW_PAD - (CONV_W - 1), 0), (0, 0)))
    return pl.pallas_call(
        functools.partial(_local_body, past),
        grid=(bsz, tlen // tm),
        in_specs=[pl.BlockSpec((1, tm, GROUP_W), lambda b, t: (b, t, 0)),
                  pl.BlockSpec((1, tm, 3 * GROUP_W), lambda b, t: (b, t, 0)),
                  pl.BlockSpec((1, _POOL_PAD, GROUP_W), lambda b, t: (b, 0, 0)),
                  pl.BlockSpec((1, _ROW_PAD, GROUP_W), lambda b, t: (b, 0, 0)),
                  _full((GROUP_W, GROUP_W)), _full((1, GROUP_W)), _full((CONV_W, GROUP_W))],
        out_specs=[pl.BlockSpec((1, tm, GROUP_W), lambda b, t: (b, t, 0)),
                   pl.BlockSpec((1, tm, GROUP_W), lambda b, t: (b, t, 0)),
                   pl.BlockSpec((1, CONV_W - 1, GROUP_W), lambda b, t: (b, 0, 0))],
        out_shape=[jax.ShapeDtypeStruct((bsz, tlen, GROUP_W), F32),
                   jax.ShapeDtypeStruct((bsz, tlen, GROUP_W), F32),
                   jax.ShapeDtypeStruct((bsz, CONV_W - 1, GROUP_W), F32)],
        scratch_shapes=[pltpu.VMEM((_POOL_PAD, GROUP_W), F32), pltpu.VMEM((_ROW_PAD, GROUP_W), F32)],
        compiler_params=_cparams(("arbitrary", "arbitrary")),
        name="local_mixers",
    )(u_pool, u_conv, pst, cst, w_bd, scale, conv_w)


def _softplus(x):
    return jnp.maximum(x, 0.0) + jnp.log1p(jnp.exp(-jnp.abs(x)))


def _prep_body(u_ref, sh_ref, mu_ref, w0_ref, w2_ref, a0_ref, a2_ref, g2_ref, kk_ref, ka_ref, hsum_ref,
               r_ref, lw_ref, k_ref, v_ref, kap_ref, b_ref, g_ref, prev_ref):
    t = pl.program_id(1)
    tm = u_ref.shape[1]

    @pl.when(t == 0)
    def _():
        prev_ref[...] = sh_ref[0]

    u = u_ref[0]
    ext = jnp.concatenate([prev_ref[...], u], axis=0)
    prev = pltpu.roll(ext, 1, 0)[_ROW_PAD:]
    prev_ref[...] = ext[tm:]
    xs = u + (prev - u) * mu_ref[...]
    r = xs[:, 0:GROUP_W]
    k = xs[:, GROUP_W:2 * GROUP_W]
    v = xs[:, 2 * GROUP_W:3 * GROUP_W]
    low = xs[:, 3 * GROUP_W:]
    w = -_softplus(-(w0_ref[...] + _dot(jnp.tanh(low), w2_ref[...]))) - 0.5
    a = jax.nn.sigmoid(a0_ref[...] + _dot(low, a2_ref[...]))
    kk = k * kk_ref[...]
    norm = jnp.sqrt(_dot_split(kk * kk, hsum_ref[...]))
    kap = kk / jnp.maximum(norm, 1e-12)
    r_ref[0] = r
    lw_ref[0] = -jnp.exp(w)
    k_ref[0] = k * (1.0 + (a - 1.0) * ka_ref[...])
    v_ref[0] = v
    kap_ref[0] = kap
    b_ref[0] = kap * a
    g_ref[0] = _dot(jax.nn.sigmoid(low), g2_ref[...])


def _rwkv_prep(u, shift_state, mu, w0, w2p, a0, a2p, g2p, k_k, k_a, hsum):
    bsz, tlen, _ = u.shape
    tm = min(TIME_TILE, tlen)
    sh = jnp.pad(shift_state[:, None, :], ((0, 0), (_ROW_PAD - 1, 0), (0, 0)))
    grp = pl.BlockSpec((1, tm, GROUP_W), lambda b, t: (b, t, 0))
    vec = _full((1, GROUP_W))
    low = _full((RWKV_LOWRANK_W, GROUP_W))
    return pl.pallas_call(
        _prep_body,
        grid=(bsz, tlen // tm),
        in_specs=[pl.BlockSpec((1, tm, RWKV_IN), lambda b, t: (b, t, 0)),
                  pl.BlockSpec((1, _ROW_PAD, RWKV_IN), lambda b, t: (b, 0, 0)),
                  _full((1, RWKV_IN)), vec, low, vec, low, low, vec, vec, _full((GROUP_W, GROUP_W))],
        out_specs=[grp] * 7,
        out_shape=[jax.ShapeDtypeStruct((bsz, tlen, GROUP_W), F32)] * 7,
        scratch_shapes=[pltpu.VMEM((_ROW_PAD, RWKV_IN), F32)],
        compiler_params=_cparams(("arbitrary", "arbitrary")),
        name="rwkv_prep",
    )(u, sh, mu, w0, w2p, a0, a2p, g2p, k_k, k_a, hsum)


_NN = ((2,), (1,))
_NT = ((2,), (2,))
_TN = ((1,), (1,))


def _bmm(a, b, dims=_NN, passes=1):
    dg = lambda x, y: lax.dot_general(x, y, (dims, ((0,), (0,))), preferred_element_type=F32)
    if passes == 1:
        return dg(a.astype(BF16), b.astype(BF16))
    ah, al = _split2(a)
    bh, bl = _split2(b)
    return dg(ah, bh) + (dg(ah, bl) + dg(al, bh))


def _unit_lower_inverse(a, c):
    row = lax.broadcasted_iota(jnp.int32, (c, c), 0)
    col = lax.broadcasted_iota(jnp.int32, (c, c), 1)
    eye = (row == col).astype(F32)

    def series(m, order):
        x = eye - m
        p = m
        n = 2
        while n < order:
            p = _bmm(p, p, passes=INV_PASSES)
            x = x + _bmm(x, p, passes=INV_PASSES)
            n *= 2
        return x

    if c == SCAN_BLOCK:
        return series(a, c)
    same_block = (row // SCAN_BLOCK) == (col // SCAN_BLOCK)
    a_diag = jnp.where(same_block, a, 0.0)
    t_diag = series(a_diag, SCAN_BLOCK)
    b = _bmm(t_diag, a - a_diag, passes=INV_PASSES)
    return _bmm(series(b, c // SCAN_BLOCK), t_diag, passes=INV_PASSES)


def _scan_body(r_ref, lw_ref, k_ref, v_ref, kap_ref, b_ref, g_ref, s0_ref, rk_ref, lnw_ref, lnb_ref,
               y_ref, so_ref, s_ref):
    t = pl.program_id(1)
    nb, c, _ = r_ref.shape

    @pl.when(t == 0)
    def _():
        s_ref[...] = s0_ref[...]

    row = lax.broadcasted_iota(jnp.int32, (c, c), 0)
    col = lax.broadcasted_iota(jnp.int32, (c, c), 1)
    incl = row >= col
    strict = row > col
    tri = incl.astype(BF16)

    def heads(x):
        return jnp.stack([x[i][:, h * HEAD_DIM:(h + 1) * HEAD_DIM]
                          for i in range(nb) for h in range(N_HEADS)])

    r_all, lw, k_all, v_all = r_ref[...], lw_ref[...], k_ref[...], v_ref[...]
    kap_all, b_all = kap_ref[...], b_ref[...]
    lw_hi, lw_lo = _split2(lw)
    lw_lo2 = (lw - lw_hi.astype(F32) - lw_lo.astype(F32)).astype(BF16)
    cum = jnp.stack([jnp.dot(tri, lw_hi[i], preferred_element_type=F32)
                     + (jnp.dot(tri, lw_lo[i], preferred_element_type=F32)
                        + jnp.dot(tri, lw_lo2[i], preferred_element_type=F32)) for i in range(nb)])
    cmid = cum[:, c // 2 - 1:c // 2, :]
    cend = cum[:, c - 1:c, :]
    e_bwd = jnp.exp(cmid - cum)
    e_end = jnp.exp(cend - cum)
    r_t = heads(r_all * jnp.exp(cum - cmid))
    kap_t = heads(kap_all * jnp.exp(cum - lw - cmid))
    k_t = heads(k_all * e_bwd)
    b_t = heads(b_all * e_bwd)
    r_0 = heads(r_all * jnp.exp(cum))
    kap_0 = heads(kap_all * jnp.exp(cum - lw))
    k_e = heads(k_all * e_end)
    b_e = heads(b_all * e_end)
    g_end = heads(jnp.exp(cend))
    v = heads(v_all)
    s = s_ref[...].reshape(nb * N_HEADS, HEAD_DIM, HEAD_DIM)

    a_kb = jnp.where(strict, _bmm(kap_t, b_t, _NT), 0.0)
    a_kk = jnp.where(strict, _bmm(kap_t, k_t, _NT), 0.0)
    a_rb = jnp.where(incl, _bmm(r_t, b_t, _NT), 0.0)
    a_rk = jnp.where(incl, _bmm(r_t, k_t, _NT), 0.0)
    t_inv = _unit_lower_inverse(a_kb, c)

    u = _bmm(t_inv, _bmm(kap_0, s, _NT, STATE_PASSES) + _bmm(a_kk, v), passes=STATE_PASSES)
    y = _bmm(r_0, s, _NT, STATE_PASSES) + _bmm(a_rk, v) - _bmm(a_rb, u)
    s_new = s * g_end + _bmm(v, k_e, _TN, STATE_PASSES) - _bmm(u, b_e, _TN, STATE_PASSES)
    s_new = s_new.reshape(nb, N_HEADS, HEAD_DIM, HEAD_DIM)
    s_ref[...] = s_new
    so_ref[...] = s_new

    mean = jnp.mean(y, axis=-1, keepdims=True)
    yc = y - mean
    var = jnp.mean(yc * yc, axis=-1, keepdims=True)
    yn = yc * lax.rsqrt(var + GN_EPS)
    bonus = jnp.sum(heads(r_all * k_all * rk_ref[...]), axis=-1, keepdims=True) * v
    for i in range(nb):
        wide = lambda x: jnp.concatenate([x[i * N_HEADS + h] for h in range(N_HEADS)], axis=-1)
        y_ref[i] = (wide(yn) * lnw_ref[...] + lnb_ref[...] + wide(bonus)) * g_ref[i]


def _rwkv_scan(r, lw, k, v, kap, b, g, s0, r_k, ln_w, ln_b):
    bsz, tlen, _ = r.shape
    c = SCAN_CHUNK
    nb = SCAN_BATCH
    grp = pl.BlockSpec((nb, c, GROUP_W), lambda b_, t: (b_, t, 0))
    st = pl.BlockSpec((nb, N_HEADS, HEAD_DIM, HEAD_DIM), lambda b_, t: (b_, 0, 0, 0))
    vec = _full((1, GROUP_W))
    return pl.pallas_call(
        _scan_body,
        grid=(bsz // nb, tlen // c),
        in_specs=[grp] * 7 + [st, vec, vec, vec],
        out_specs=[grp, st],
        out_shape=[jax.ShapeDtypeStruct((bsz, tlen, GROUP_W), F32),
                   jax.ShapeDtypeStruct((bsz, N_HEADS, HEAD_DIM, HEAD_DIM), F32)],
        scratch_shapes=[pltpu.VMEM((nb, N_HEADS, HEAD_DIM, HEAD_DIM), F32)],
        compiler_params=_cparams(("arbitrary", "arbitrary")),
        name="rwkv_scan",
    )(r, lw, k, v, kap, b, g, s0, r_k, ln_w, ln_b)


def _attn_body(past, q_ref, k_ref, v_ref, o_ref):
    qi = pl.program_id(1)
    tq = q_ref.shape[1]
    tk = ATTN_TILE
    q0 = past + qi * tq
    diag = q0 // tk
    scale = HEAD_DIM ** -0.5 * LOG2_E

    qs = [(q_ref[0, :, h * HEAD_DIM:(h + 1) * HEAD_DIM] * scale).astype(BF16) for h in range(N_HEADS)]
    later = (lax.broadcasted_iota(jnp.int32, (tk, tk), 0)
             > lax.broadcasted_iota(jnp.int32, (tk, tk), 1)).astype(BF16)

    def tile(j, carry, masked):
        accs, runs = carry
        start = pl.multiple_of(j * tk, tk)
        kt = k_ref[0, pl.ds(start, tk), :].astype(BF16)
        vt = v_ref[0, pl.ds(start, tk), :].astype(BF16)
        if masked:
            qpos = q0 + lax.broadcasted_iota(jnp.int32, (tq, tk), 0)
            kpos = j * tk + lax.broadcasted_iota(jnp.int32, (tq, tk), 1)
            vis = kpos < qpos
        new_accs, new_runs = [], []
        for h in range(N_HEADS):
            sl = slice(h * HEAD_DIM, (h + 1) * HEAD_DIM)
            z = _dot_nt(qs[h], kt[:, sl])
            sp = jnp.log2(1.0 + jnp.exp2(-jnp.abs(z)))
            log_take = jnp.minimum(z, 0.0) - sp
            log_keep = log_take - z
            if masked:
                log_keep = jnp.where(vis, log_keep, 0.0)
            after = _dot_split(log_keep, later) + runs[h]
            p = jnp.exp2(log_take + after)
            if masked:
                p = jnp.where(vis, p, 0.0)
            new_accs.append(accs[h] + jnp.dot(p.astype(BF16), vt[:, sl], preferred_element_type=F32))
            new_runs.append(runs[h] + jnp.sum(log_keep, axis=-1, keepdims=True))
        return tuple(new_accs), tuple(new_runs)

    init = (tuple(jnp.zeros((tq, HEAD_DIM), F32) for _ in range(N_HEADS)),
            tuple(jnp.zeros((tq, 1), F32) for _ in range(N_HEADS)))
    carry = tile(diag, init, True)
    accs, _ = lax.fori_loop(0, diag, lambda i, cr: tile(diag - 1 - i, cr, False), carry)
    o_ref[0] = jnp.concatenate(accs, axis=-1)


def _sb_attn(q, k_all, v_all, past):
    bsz, tlen, _ = q.shape
    tq = min(ATTN_TILE, tlen)
    slen = k_all.shape[1]
    kv = pl.BlockSpec((1, slen, GROUP_W), lambda b, i: (b, 0, 0))
    return pl.pallas_call(
        functools.partial(_attn_body, past),
        grid=(bsz, tlen // tq),
        in_specs=[pl.BlockSpec((1, tq, GROUP_W), lambda b, i: (b, i, 0)), kv, kv],
        out_specs=pl.BlockSpec((1, tq, GROUP_W), lambda b, i: (b, i, 0)),
        out_shape=jax.ShapeDtypeStruct((bsz, tlen, GROUP_W), F32),
        compiler_params=_cparams(("arbitrary", "arbitrary")),
        name="sb_attn",
    )(q, k_all, v_all)


def _pad_time(x, mult):
    pad = (-x.shape[1]) % mult
    return x if pad == 0 else jnp.pad(x, ((0, 0), (0, pad), (0, 0)))


def _prep_params(prm):
    out = []
    depth = prm["w_in"].shape[0]
    head_of = jnp.arange(GROUP_W) // HEAD_DIM
    hsum = (head_of[:, None] == head_of[None, :]).astype(BF16)
    row2 = lambda v: v.reshape(1, -1)
    for l in range(depth):
        w_bd = jax.scipy.linalg.block_diag(*[prm["pool_w"][l, i] for i in range(len(POOL_WINDOWS))])
        lowpad = lambda w, lo: jnp.zeros((RWKV_LOWRANK_W, GROUP_W), F32).at[lo:lo + w.shape[0]].set(w)
        out.append(dict(
            g=[row2(prm["norm_g"][l, i]) for i in range(6)],
            wg=[prm["ffn_w_gate"][l, i].astype(BF16) for i in range(2)],
            wu=[prm["ffn_w_up"][l, i].astype(BF16) for i in range(2)],
            wd=[prm["ffn_w_down"][l, i].astype(BF16) for i in range(2)],
            w_in=prm["w_in"][l].astype(BF16),
            w_out=prm["w_out"][l].astype(BF16),
            w_bd=w_bd.astype(BF16),
            pool_scale=row2(prm["pool_scale"][l]),
            conv_w=prm["conv_w"][l],
            mu=row2(prm["rwkv_mu"][l]),
            w0=row2(prm["rwkv_w0"][l]),
            w2p=lowpad(prm["rwkv_w2"][l], 0).astype(BF16),
            a0=row2(prm["rwkv_a0"][l]),
            a2p=lowpad(prm["rwkv_a2"][l], 32).astype(BF16),
            g2p=lowpad(prm["rwkv_g2"][l], 64).astype(BF16),
            k_k=row2(prm["rwkv_k_k"][l]),
            k_a=row2(prm["rwkv_k_a"][l]),
            r_k=row2(prm["rwkv_r_k"][l]),
            ln_w=row2(prm["rwkv_ln_w"][l]),
            ln_b=row2(prm["rwkv_ln_b"][l]),
            hsum=hsum,
        ))
    return out


def _trunk(x, k_past, v_past, wkv0, shift0, conv0, pool0, layers, past):
    bsz, tlen, _ = x.shape
    n = bsz * tlen
    x = x.reshape(n, D_MODEL)
    ks, vs, wkvs, shifts, convs, pools = [], [], [], [], [], []
    for l, p in enumerate(layers):
        x = _ffn(x, p["g"][0], p["g"][1], p["wg"][0], p["wu"][0], p["wd"][0])
        u_pool, u_rwkv, u_conv, q, k, v = _inproj(x, p["g"][2], p["w_in"])
        seq = lambda a: a.reshape(bsz, tlen, -1)
        u_pool, u_rwkv, u_conv, q, k, v = map(seq, (u_pool, u_rwkv, u_conv, q, k, v))

        y_pool, y_conv, conv_new = _local(u_pool, u_conv, pool0[l], conv0[l], p["w_bd"],
                                          p["pool_scale"], p["conv_w"], past)

        prep = _rwkv_prep(u_rwkv, shift0[l], p["mu"], p["w0"], p["w2p"], p["a0"], p["a2p"], p["g2p"],
                          p["k_k"], p["k_a"], p["hsum"])
        prep = [_pad_time(a, SCAN_CHUNK) for a in prep]
        y_rwkv, wkv_new = _rwkv_scan(*prep, wkv0[l], p["r_k"], p["ln_w"], p["ln_b"])
        y_rwkv = y_rwkv[:, :tlen]

        if past:
            k_all = jnp.concatenate([k_past[l].reshape(bsz, past, GROUP_W), k], axis=1)
            v_all = jnp.concatenate([v_past[l].reshape(bsz, past, GROUP_W), v], axis=1)
        else:
            k_all, v_all = k, v
        y_sb = _sb_attn(q, _pad_time(k_all, ATTN_TILE), _pad_time(v_all, ATTN_TILE), past)

        flat = lambda a: a.reshape(n, GROUP_W)
        x = _outproj(x, flat(y_pool), flat(y_rwkv), flat(y_conv), flat(y_sb), p["w_out"], p["g"][3])
        x = _ffn(x, p["g"][4], p["g"][5], p["wg"][1], p["wu"][1], p["wd"][1])

        ks.append(k.reshape(bsz, tlen, N_HEADS, HEAD_DIM))
        vs.append(v.reshape(bsz, tlen, N_HEADS, HEAD_DIM))
        wkvs.append(wkv_new)
        shifts.append(u_rwkv[:, -1])
        convs.append(conv_new)
        pools.append(u_pool[:, -POOL_STATE:])
    return (x.reshape(bsz, tlen, D_MODEL), jnp.stack(ks), jnp.stack(vs), jnp.stack(wkvs),
            jnp.stack(shifts), jnp.stack(convs), jnp.stack(pools))


def kernel(x_prompt, x_sample, cache_sb_k, cache_sb_v, state_wkv, state_shift, state_conv, state_pool,
           norm_g, ffn_w_gate, ffn_w_up, ffn_w_down, w_in, w_out, pool_w, pool_scale,
           rwkv_mu, rwkv_w0, rwkv_w2, rwkv_a0, rwkv_a2, rwkv_g2, rwkv_k_k, rwkv_k_a, rwkv_r_k,
           rwkv_ln_w, rwkv_ln_b, conv_w):
    layers = _prep_params(dict(
        norm_g=norm_g, ffn_w_gate=ffn_w_gate, ffn_w_up=ffn_w_up, ffn_w_down=ffn_w_down, w_in=w_in,
        w_out=w_out, pool_w=pool_w, pool_scale=pool_scale, rwkv_mu=rwkv_mu, rwkv_w0=rwkv_w0,
        rwkv_w2=rwkv_w2, rwkv_a0=rwkv_a0, rwkv_a2=rwkv_a2, rwkv_g2=rwkv_g2, rwkv_k_k=rwkv_k_k,
        rwkv_k_a=rwkv_k_a, rwkv_r_k=rwkv_r_k, rwkv_ln_w=rwkv_ln_w, rwkv_ln_b=rwkv_ln_b, conv_w=conv_w))
    depth = w_in.shape[0]
    bsz = x_prompt.shape[0]
    z_wkv = jnp.zeros((depth, bsz, N_HEADS, HEAD_DIM, HEAD_DIM), F32)
    z_shift = jnp.zeros((depth, bsz, RWKV_IN), F32)
    z_conv = jnp.zeros((depth, bsz, CONV_W - 1, GROUP_W), F32)
    z_pool = jnp.zeros((depth, bsz, POOL_STATE, GROUP_W), F32)
    prompt = _trunk(x_prompt, None, None, z_wkv, z_shift, z_conv, z_pool, layers, 0)
    sample = _trunk(x_sample, cache_sb_k, cache_sb_v, state_wkv, state_shift, state_conv, state_pool,
                    layers, cache_sb_k.shape[2])
    return (prompt[0], sample[0]) + prompt[1:] + sample[1:]
```

```python
import functools

import jax
import jax.numpy as jnp
from jax import lax
from jax.experimental import pallas as pl
from jax.experimental.pallas import tpu as pltpu

F32 = jnp.float32
BF16 = jnp.bfloat16

D_MODEL = 1024
GROUP_W = 256
HEAD_DIM = 64
N_HEADS = 4
POOL_WINDOWS = (2, 4, 8, 16)
POOL_STATE = 15
CONV_W = 3
RWKV_IN = 896
RWKV_LOWRANK_W = 128
D_FF = 2816
RMS_EPS = 1e-6
GN_EPS = 64e-5
LOG2_E = 1.4426950408889634

FF_CHUNK = 256
TOKEN_TILE = 512
TIME_TILE = 512
SCAN_CHUNK = 64
SCAN_BLOCK = 16
SCAN_BATCH = 4
INV_PASSES = 3
STATE_PASSES = 3
ATTN_TILE = 256
VMEM_LIMIT = 56 * 1024 * 1024


def _cparams(sem):
    return pltpu.CompilerParams(dimension_semantics=sem, vmem_limit_bytes=VMEM_LIMIT)


def _rms(x, g):
    return x * lax.rsqrt(jnp.mean(x * x, axis=-1, keepdims=True) + RMS_EPS) * g


def _dot(a, b):
    return jnp.dot(a.astype(BF16), b.astype(BF16), preferred_element_type=F32)


def _dot_nt(a, b):
    return lax.dot_general(a, b, (((1,), (1,)), ((), ())), preferred_element_type=F32)


def _split2(a):
    hi = a.astype(BF16)
    return hi, (a - hi.astype(F32)).astype(BF16)


def _dot_split(a, ones_b):
    hi, lo = _split2(a)
    return (jnp.dot(hi, ones_b, preferred_element_type=F32)
            + jnp.dot(lo, ones_b, preferred_element_type=F32))


def _full(shape):
    return pl.BlockSpec(shape, lambda *_: (0,) * len(shape))


def _ffn_body(x_ref, gpre_ref, gpost_ref, wg_ref, wu_ref, wd_ref, o_ref, a_ref):
    x = x_ref[...]
    hb = _rms(x, gpre_ref[...]).astype(BF16)
    for j in range(D_FF // FF_CHUNK):
        sl = slice(j * FF_CHUNK, (j + 1) * FF_CHUNK)
        g = jnp.dot(hb, wg_ref[:, sl], preferred_element_type=F32)
        u = jnp.dot(hb, wu_ref[:, sl], preferred_element_type=F32)
        a_ref[:, sl] = (g * jax.nn.sigmoid(g) * u).astype(BF16)
    y = jnp.dot(a_ref[...], wd_ref[...], preferred_element_type=F32)
    o_ref[...] = x + 0.5 * _rms(y, gpost_ref[...])


def _ffn(x, g_pre, g_post, wg, wu, wd):
    n = x.shape[0]
    tm = min(TOKEN_TILE, n)
    row = pl.BlockSpec((tm, D_MODEL), lambda i: (i, 0))
    return pl.pallas_call(
        _ffn_body,
        grid=(n // tm,),
        in_specs=[row, _full((1, D_MODEL)), _full((1, D_MODEL)),
                  _full((D_MODEL, D_FF)), _full((D_MODEL, D_FF)), _full((D_FF, D_MODEL))],
        out_specs=row,
        out_shape=jax.ShapeDtypeStruct((n, D_MODEL), F32),
        scratch_shapes=[pltpu.VMEM((tm, D_FF), BF16)],
        compiler_params=_cparams(("arbitrary",)),
        name="ffn",
    )(x, g_pre, g_post, wg, wu, wd)


_PROJ_EDGES = (0, 256, 1152, 1920, 2176, 2432, 2688)


def _inproj_body(x_ref, g_ref, w_ref, *o_refs):
    hb = _rms(x_ref[...], g_ref[...]).astype(BF16)
    for o_ref, lo, hi in zip(o_refs, _PROJ_EDGES[:-1], _PROJ_EDGES[1:]):
        o_ref[...] = jnp.dot(hb, w_ref[:, lo:hi], preferred_element_type=F32)


def _inproj(x, g, w_in):
    n = x.shape[0]
    tm = min(TOKEN_TILE, n)
    widths = [hi - lo for lo, hi in zip(_PROJ_EDGES[:-1], _PROJ_EDGES[1:])]
    return pl.pallas_call(
        _inproj_body,
        grid=(n // tm,),
        in_specs=[pl.BlockSpec((tm, D_MODEL), lambda i: (i, 0)), _full((1, D_MODEL)),
                  _full((D_MODEL, _PROJ_EDGES[-1]))],
        out_specs=[pl.BlockSpec((tm, w), lambda i: (i, 0)) for w in widths],
        out_shape=[jax.ShapeDtypeStruct((n, w), F32) for w in widths],
        compiler_params=_cparams(("arbitrary",)),
        name="inproj",
    )(x, g, w_in)


def _outproj_body(x_ref, a_ref, b_ref, c_ref, d_ref, w_ref, g_ref, o_ref):
    y = _dot(a_ref[...], w_ref[0 * GROUP_W:1 * GROUP_W, :])
    y += _dot(b_ref[...], w_ref[1 * GROUP_W:2 * GROUP_W, :])
    y += _dot(c_ref[...], w_ref[2 * GROUP_W:3 * GROUP_W, :])
    y += _dot(d_ref[...], w_ref[3 * GROUP_W:4 * GROUP_W, :])
    o_ref[...] = x_ref[...] + _rms(y, g_ref[...])


def _outproj(x, ya, yb, yc, yd, w_out, g):
    n = x.shape[0]
    tm = min(TOKEN_TILE, n)
    row = pl.BlockSpec((tm, D_MODEL), lambda i: (i, 0))
    grp = pl.BlockSpec((tm, GROUP_W), lambda i: (i, 0))
    return pl.pallas_call(
        _outproj_body,
        grid=(n // tm,),
        in_specs=[row, grp, grp, grp, grp, _full((D_MODEL, D_MODEL)), _full((1, D_MODEL))],
        out_specs=row,
        out_shape=jax.ShapeDtypeStruct((n, D_MODEL), F32),
        compiler_params=_cparams(("arbitrary",)),
        name="outproj",
    )(x, ya, yb, yc, yd, w_out, g)


_POOL_PAD = 16
_ROW_PAD = 8


def _local_body(past, up_ref, uc_ref, pst_ref, cst_ref, wbd_ref, scale_ref, cw_ref,
                yp_ref, yc_ref, cso_ref, pprev_ref, cprev_ref):
    t = pl.program_id(1)
    tm = up_ref.shape[1]

    @pl.when(t == 0)
    def _():
        pprev_ref[...] = pst_ref[0]
        cprev_ref[...] = cst_ref[0]

    u = up_ref[0]
    ext = jnp.concatenate([pprev_ref[...], u], axis=0)
    s2 = ext + pltpu.roll(ext, 1, 0)
    s4 = s2 + pltpu.roll(s2, 2, 0)
    s8 = s4 + pltpu.roll(s4, 4, 0)
    s16 = s8 + pltpu.roll(s8, 8, 0)
    grp = lax.broadcasted_iota(jnp.int32, (tm, GROUP_W), 1) // (GROUP_W // len(POOL_WINDOWS))
    pos1 = past + t * tm + lax.broadcasted_iota(jnp.int32, (tm, GROUP_W), 0) + 1
    win = jnp.where(grp == 0, s2[_POOL_PAD:], jnp.where(grp == 1, s4[_POOL_PAD:],
                    jnp.where(grp == 2, s8[_POOL_PAD:], s16[_POOL_PAD:])))
    width = jnp.where(grp == 0, 2, jnp.where(grp == 1, 4, jnp.where(grp == 2, 8, 16)))
    cnt = jnp.minimum(width, pos1).astype(F32)
    d = win / cnt - u
    yp_ref[0] = _dot(d, wbd_ref[...]) * scale_ref[...]
    pprev_ref[...] = ext[tm:]

    uc = uc_ref[0]
    b = uc[:, 0:GROUP_W]
    z = uc[:, GROUP_W:2 * GROUP_W] * uc[:, 2 * GROUP_W:3 * GROUP_W]
    extz = jnp.concatenate([cprev_ref[...], z], axis=0)
    y = (pltpu.roll(extz, 2, 0)[_ROW_PAD:] * cw_ref[0:1, :]
         + pltpu.roll(extz, 1, 0)[_ROW_PAD:] * cw_ref[1:2, :]
         + z * cw_ref[2:3, :])
    yc_ref[0] = b * y
    cprev_ref[...] = extz[tm:]
    cso_ref[0] = extz[tm:][_ROW_PAD - (CONV_W - 1):]


def _local(u_pool, u_conv, pool_state, conv_state, w_bd, scale, conv_w, past):
    bsz, tlen, _ = u_pool.shape
    tm = min(TIME_TILE, tlen)
    pst = jnp.pad(pool_state, ((0, 0), (_POOL_PAD - POOL_STATE, 0), (0, 0)))
    cst = jnp.pad(conv_state, ((0, 0), (_ROW_PAD - (CONV_W - 1), 0), (0, 0)))
    return pl.pallas_call(
        functools.partial(_local_body, past),
        grid=(bsz, tlen // tm),
        in_specs=[pl.BlockSpec((1, tm, GROUP_W), lambda b, t: (b, t, 0)),
                  pl.BlockSpec((1, tm, 3 * GROUP_W), lambda b, t: (b, t, 0)),
                  pl.BlockSpec((1, _POOL_PAD, GROUP_W), lambda b, t: (b, 0, 0)),
                  pl.BlockSpec((1, _ROW_PAD, GROUP_W), lambda b, t: (b, 0, 0)),
                  _full((GROUP_W, GROUP_W)), _full((1, GROUP_W)), _full((CONV_W, GROUP_W))],
        out_specs=[pl.BlockSpec((1, tm, GROUP_W), lambda b, t: (b, t, 0)),
                   pl.BlockSpec((1, tm, GROUP_W), lambda b, t: (b, t, 0)),
                   pl.BlockSpec((1, CONV_W - 1, GROUP_W), lambda b, t: (b, 0, 0))],
        out_shape=[jax.ShapeDtypeStruct((bsz, tlen, GROUP_W), F32),
                   jax.ShapeDtypeStruct((bsz, tlen, GROUP_W), F32),
                   jax.ShapeDtypeStruct((bsz, CONV_W - 1, GROUP_W), F32)],
        scratch_shapes=[pltpu.VMEM((_POOL_PAD, GROUP_W), F32), pltpu.VMEM((_ROW_PAD, GROUP_W), F32)],
        compiler_params=_cparams(("arbitrary", "arbitrary")),
        name="local_mixers",
    )(u_pool, u_conv, pst, cst, w_bd, scale, conv_w)


def _softplus(x):
    return jnp.maximum(x, 0.0) + jnp.log1p(jnp.exp(-jnp.abs(x)))


def _prep_body(u_ref, sh_ref, mu_ref, w0_ref, w2_ref, a0_ref, a2_ref, g2_ref, kk_ref, ka_ref, hsum_ref,
               r_ref, lw_ref, k_ref, v_ref, kap_ref, b_ref, g_ref, prev_ref):
    t = pl.program_id(1)
    tm = u_ref.shape[1]

    @pl.when(t == 0)
    def _():
        prev_ref[...] = sh_ref[0]

    u = u_ref[0]
    ext = jnp.concatenate([prev_ref[...], u], axis=0)
    prev = pltpu.roll(ext, 1, 0)[_ROW_PAD:]
    prev_ref[...] = ext[tm:]
    xs = u + (prev - u) * mu_ref[...]
    r = xs[:, 0:GROUP_W]
    k = xs[:, GROUP_W:2 * GROUP_W]
    v = xs[:, 2 * GROUP_W:3 * GROUP_W]
    low = xs[:, 3 * GROUP_W:]
    w = -_softplus(-(w0_ref[...] + _dot(jnp.tanh(low), w2_ref[...]))) - 0.5
    a = jax.nn.sigmoid(a0_ref[...] + _dot(low, a2_ref[...]))
    kk = k * kk_ref[...]
    norm = jnp.sqrt(_dot_split(kk * kk, hsum_ref[...]))
    kap = kk / jnp.maximum(norm, 1e-12)
    r_ref[0] = r
    lw_ref[0] = -jnp.exp(w)
    k_ref[0] = k * (1.0 + (a - 1.0) * ka_ref[...])
    v_ref[0] = v
    kap_ref[0] = kap
    b_ref[0] = kap * a
    g_ref[0] = _dot(jax.nn.sigmoid(low), g2_ref[...])


def _rwkv_prep(u, shift_state, mu, w0, w2p, a0, a2p, g2p, k_k, k_a, hsum):
    bsz, tlen, _ = u.shape
    tm = min(TIME_TILE, tlen)
    sh = jnp.pad(shift_state[:, None, :], ((0, 0), (_ROW_PAD - 1, 0), (0, 0)))
    grp = pl.BlockSpec((1, tm, GROUP_W), lambda b, t: (b, t, 0))
    vec = _full((1, GROUP_W))
    low = _full((RWKV_LOWRANK_W, GROUP_W))
    return pl.pallas_call(
        _prep_body,
        grid=(bsz, tlen // tm),
        in_specs=[pl.BlockSpec((1, tm, RWKV_IN), lambda b, t: (b, t, 0)),
                  pl.BlockSpec((1, _ROW_PAD, RWKV_IN), lambda b, t: (b, 0, 0)),
                  _full((1, RWKV_IN)), vec, low, vec, low, low, vec, vec, _full((GROUP_W, GROUP_W))],
        out_specs=[grp] * 7,
        out_shape=[jax.ShapeDtypeStruct((bsz, tlen, GROUP_W), F32)] * 7,
        scratch_shapes=[pltpu.VMEM((_ROW_PAD, RWKV_IN), F32)],
        compiler_params=_cparams(("arbitrary", "arbitrary")),
        name="rwkv_prep",
    )(u, sh, mu, w0, w2p, a0, a2p, g2p, k_k, k_a, hsum)


_NN = ((2,), (1,))
_NT = ((2,), (2,))
_TN = ((1,), (1,))


def _bmm(a, b, dims=_NN, passes=1):
    dg = lambda x, y: lax.dot_general(x, y, (dims, ((0,), (0,))), preferred_element_type=F32)
    if passes == 1:
        return dg(a.astype(BF16), b.astype(BF16))
    ah, al = _split2(a)
    bh, bl = _split2(b)
    return dg(ah, bh) + (dg(ah, bl) + dg(al, bh))


def _unit_lower_inverse(a, c):
    row = lax.broadcasted_iota(jnp.int32, (c, c), 0)
    col = lax.broadcasted_iota(jnp.int32, (c, c), 1)
    eye = (row == col).astype(F32)

    def series(m, order):
        x = eye - m
        p = m
        n = 2
        while n < order:
            p = _bmm(p, p, passes=INV_PASSES)
            x = x + _bmm(x, p, passes=INV_PASSES)
            n *= 2
        return x

    if c == SCAN_BLOCK:
        return series(a, c)
    same_block = (row // SCAN_BLOCK) == (col // SCAN_BLOCK)
    a_diag = jnp.where(same_block, a, 0.0)
    t_diag = series(a_diag, SCAN_BLOCK)
    b = _bmm(t_diag, a - a_diag, passes=INV_PASSES)
    return _bmm(series(b, c // SCAN_BLOCK), t_diag, passes=INV_PASSES)


def _scan_body(r_ref, lw_ref, k_ref, v_ref, kap_ref, b_ref, g_ref, s0_ref, rk_ref, lnw_ref, lnb_ref,
               y_ref, so_ref, s_ref):
    t = pl.program_id(1)
    nb, c, _ = r_ref.shape

    @pl.when(t == 0)
    def _():
        s_ref[...] = s0_ref[...]

    row = lax.broadcasted_iota(jnp.int32, (c, c), 0)
    col = lax.broadcasted_iota(jnp.int32, (c, c), 1)
    incl = row >= col
    strict = row > col
    tri = incl.astype(BF16)

    def heads(x):
        return jnp.stack([x[i][:, h * HEAD_DIM:(h + 1) * HEAD_DIM]
                          for i in range(nb) for h in range(N_HEADS)])

    r_all, lw, k_all, v_all = r_ref[...], lw_ref[...], k_ref[...], v_ref[...]
    kap_all, b_all = kap_ref[...], b_ref[...]
    lw_hi, lw_lo = _split2(lw)
    lw_lo2 = (lw - lw_hi.astype(F32) - lw_lo.astype(F32)).astype(BF16)
    cum = jnp.stack([jnp.dot(tri, lw_hi[i], preferred_element_type=F32)
                     + (jnp.dot(tri, lw_lo[i], preferred_element_type=F32)
                        + jnp.dot(tri, lw_lo2[i], preferred_element_type=F32)) for i in range(nb)])
    cmid = cum[:, c // 2 - 1:c // 2, :]
    cend = cum[:, c - 1:c, :]
    e_bwd = jnp.exp(cmid - cum)
    e_end = jnp.exp(cend - cum)
    r_t = heads(r_all * jnp.exp(cum - cmid))
    kap_t = heads(kap_all * jnp.exp(cum - lw - cmid))
    k_t = heads(k_all * e_bwd)
    b_t = heads(b_all * e_bwd)
    r_0 = heads(r_all * jnp.exp(cum))
    kap_0 = heads(kap_all * jnp.exp(cum - lw))
    k_e = heads(k_all * e_end)
    b_e = heads(b_all * e_end)
    g_end = heads(jnp.exp(cend))
    v = heads(v_all)
    s = s_ref[...].reshape(nb * N_HEADS, HEAD_DIM, HEAD_DIM)

    a_kb = jnp.where(strict, _bmm(kap_t, b_t, _NT), 0.0)
    a_kk = jnp.where(strict, _bmm(kap_t, k_t, _NT), 0.0)
    a_rb = jnp.where(incl, _bmm(r_t, b_t, _NT), 0.0)
    a_rk = jnp.where(incl, _bmm(r_t, k_t, _NT), 0.0)
    t_inv = _unit_lower_inverse(a_kb, c)

    u = _bmm(t_inv, _bmm(kap_0, s, _NT, STATE_PASSES) + _bmm(a_kk, v), passes=STATE_PASSES)
    y = _bmm(r_0, s, _NT, STATE_PASSES) + _bmm(a_rk, v) - _bmm(a_rb, u)
    s_new = s * g_end + _bmm(v, k_e, _TN, STATE_PASSES) - _bmm(u, b_e, _TN, STATE_PASSES)
    s_new = s_new.reshape(nb, N_HEADS, HEAD_DIM, HEAD_DIM)
    s_ref[...] = s_new
    so_ref[...] = s_new

    mean = jnp.mean(y, axis=-1, keepdims=True)
    yc = y - mean
    var = jnp.mean(yc * yc, axis=-1, keepdims=True)
    yn = yc * lax.rsqrt(var + GN_EPS)
    bonus = jnp.sum(heads(r_all * k_all * rk_ref[...]), axis=-1, keepdims=True) * v
    for i in range(nb):
        wide = lambda x: jnp.concatenate([x[i * N_HEADS + h] for h in range(N_HEADS)], axis=-1)
        y_ref[i] = (wide(yn) * lnw_ref[...] + lnb_ref[...] + wide(bonus)) * g_ref[i]


def _rwkv_scan(r, lw, k, v, kap, b, g, s0, r_k, ln_w, ln_b):
    bsz, tlen, _ = r.shape
    c = SCAN_CHUNK
    nb = SCAN_BATCH
    grp = pl.BlockSpec((nb, c, GROUP_W), lambda b_, t: (b_, t, 0))
    st = pl.BlockSpec((nb, N_HEADS, HEAD_DIM, HEAD_DIM), lambda b_, t: (b_, 0, 0, 0))
    vec = _full((1, GROUP_W))
    return pl.pallas_call(
        _scan_body,
        grid=(bsz // nb, tlen // c),
        in_specs=[grp] * 7 + [st, vec, vec, vec],
        out_specs=[grp, st],
        out_shape=[jax.ShapeDtypeStruct((bsz, tlen, GROUP_W), F32),
                   jax.ShapeDtypeStruct((bsz, N_HEADS, HEAD_DIM, HEAD_DIM), F32)],
        scratch_shapes=[pltpu.VMEM((nb, N_HEADS, HEAD_DIM, HEAD_DIM), F32)],
        compiler_params=_cparams(("arbitrary", "arbitrary")),
        name="rwkv_scan",
    )(r, lw, k, v, kap, b, g, s0, r_k, ln_w, ln_b)


def _attn_body(past, q_ref, k_ref, v_ref, o_ref, kb_ref, vb_ref):
    qi = pl.program_id(1)
    tq = q_ref.shape[1]
    tk = ATTN_TILE
    q0 = past + qi * tq
    diag = q0 // tk
    scale = HEAD_DIM ** -0.5 * LOG2_E

    @pl.when(qi == 0)
    def _():
        kb_ref[...] = k_ref[0].astype(BF16)
        vb_ref[...] = v_ref[0].astype(BF16)

    heads = [slice(h * HEAD_DIM, (h + 1) * HEAD_DIM) for h in range(N_HEADS)]
    qs = [(q_ref[0, :, sl] * scale).astype(BF16) for sl in heads]
    later = (lax.broadcasted_iota(jnp.int32, (2 * tk, tk), 0) % tk
             > lax.broadcasted_iota(jnp.int32, (2 * tk, tk), 1)).astype(BF16)

    def tile(j, carry, masked):
        accs, runs = carry
        start = pl.multiple_of(j * tk, tk)
        kt = kb_ref[pl.ds(start, tk), :]
        vt = vb_ref[pl.ds(start, tk), :]
        z = jnp.concatenate([_dot_nt(qs[h], kt[:, heads[h]]) for h in range(N_HEADS)], axis=0)
        neg_abs = pltpu.bitcast(pltpu.bitcast(z, jnp.uint32) | jnp.uint32(0x80000000), F32)
        sp = jnp.log2(1.0 + jnp.exp2(neg_abs))
        log_take = jnp.minimum(z, 0.0) - sp
        log_keep = log_take - z
        if masked:
            row = lax.broadcasted_iota(jnp.int32, (N_HEADS * tq, tk), 0) % tq
            vis = j * tk + lax.broadcasted_iota(jnp.int32, (N_HEADS * tq, tk), 1) < q0 + row
            log_keep = jnp.where(vis, log_keep, 0.0)
        hi, lo = _split2(log_keep)
        after = jnp.dot(jnp.concatenate([hi, lo], axis=1), later,
                        preferred_element_type=F32)
        p = jnp.exp2(log_take + after)
        if masked:
            p = jnp.where(vis, p, 0.0)
        p = p.astype(BF16)
        total = after[:, 0:1] + log_keep[:, 0:1]
        new_accs, new_runs = [], []
        for h in range(N_HEADS):
            rows = slice(h * tq, (h + 1) * tq)
            part = jnp.dot(p[rows], vt[:, heads[h]], preferred_element_type=F32)
            new_accs.append(accs[h] + jnp.exp2(runs[h]) * part)
            new_runs.append(runs[h] + total[rows])
        return tuple(new_accs), tuple(new_runs)

    init = (tuple(jnp.zeros((tq, HEAD_DIM), F32) for _ in range(N_HEADS)),
            tuple(jnp.zeros((tq, 1), F32) for _ in range(N_HEADS)))
    carry = tile(diag, init, True)
    accs, _ = lax.fori_loop(0, diag, lambda i, cr: tile(diag - 1 - i, cr, False), carry)
    o_ref[0] = jnp.concatenate(accs, axis=-1)


def _sb_attn(q, k_all, v_all, past):
    bsz, tlen, _ = q.shape
    tq = min(ATTN_TILE, tlen)
    slen = k_all.shape[1]
    kv = pl.BlockSpec((1, slen, GROUP_W), lambda b, i: (b, 0, 0))
    return pl.pallas_call(
        functools.partial(_attn_body, past),
        grid=(bsz, tlen // tq),
        in_specs=[pl.BlockSpec((1, tq, GROUP_W), lambda b, i: (b, i, 0)), kv, kv],
        out_specs=pl.BlockSpec((1, tq, GROUP_W), lambda b, i: (b, i, 0)),
        out_shape=jax.ShapeDtypeStruct((bsz, tlen, GROUP_W), F32),
        scratch_shapes=[pltpu.VMEM((slen, GROUP_W), BF16), pltpu.VMEM((slen, GROUP_W), BF16)],
        compiler_params=_cparams(("arbitrary", "arbitrary")),
        name="sb_attn",
    )(q, k_all, v_all)


def _pad_time(x, mult):
    pad = (-x.shape[1]) % mult
    return x if pad == 0 else jnp.pad(x, ((0, 0), (0, pad), (0, 0)))


def _prep_params(prm):
    out = []
    depth = prm["w_in"].shape[0]
    head_of = jnp.arange(GROUP_W) // HEAD_DIM
    hsum = (head_of[:, None] == head_of[None, :]).astype(BF16)
    row2 = lambda v: v.reshape(1, -1)
    for l in range(depth):
        w_bd = jax.scipy.linalg.block_diag(*[prm["pool_w"][l, i] for i in range(len(POOL_WINDOWS))])
        lowpad = lambda w, lo: jnp.zeros((RWKV_LOWRANK_W, GROUP_W), F32).at[lo:lo + w.shape[0]].set(w)
        out.append(dict(
            g=[row2(prm["norm_g"][l, i]) for i in range(6)],
            wg=[prm["ffn_w_gate"][l, i].astype(BF16) for i in range(2)],
            wu=[prm["ffn_w_up"][l, i].astype(BF16) for i in range(2)],
            wd=[prm["ffn_w_down"][l, i].astype(BF16) for i in range(2)],
            w_in=prm["w_in"][l].astype(BF16),
            w_out=prm["w_out"][l].astype(BF16),
            w_bd=w_bd.astype(BF16),
            pool_scale=row2(prm["pool_scale"][l]),
            conv_w=prm["conv_w"][l],
            mu=row2(prm["rwkv_mu"][l]),
            w0=row2(prm["rwkv_w0"][l]),
            w2p=lowpad(prm["rwkv_w2"][l], 0).astype(BF16),
            a0=row2(prm["rwkv_a0"][l]),
            a2p=lowpad(prm["rwkv_a2"][l], 32).astype(BF16),
            g2p=lowpad(prm["rwkv_g2"][l], 64).astype(BF16),
            k_k=row2(prm["rwkv_k_k"][l]),
            k_a=row2(prm["rwkv_k_a"][l]),
            r_k=row2(prm["rwkv_r_k"][l]),
            ln_w=row2(prm["rwkv_ln_w"][l]),
            ln_b=row2(prm["rwkv_ln_b"][l]),
            hsum=hsum,
        ))
    return out


def _trunk(x, k_past, v_past, wkv0, shift0, conv0, pool0, layers, past):
    bsz, tlen, _ = x.shape
    n = bsz * tlen
    x = x.reshape(n, D_MODEL)
    ks, vs, wkvs, shifts, convs, pools = [], [], [], [], [], []
    for l, p in enumerate(layers):
        x = _ffn(x, p["g"][0], p["g"][1], p["wg"][0], p["wu"][0], p["wd"][0])
        u_pool, u_rwkv, u_conv, q, k, v = _inproj(x, p["g"][2], p["w_in"])
        seq = lambda a: a.reshape(bsz, tlen, -1)
        u_pool, u_rwkv, u_conv, q, k, v = map(seq, (u_pool, u_rwkv, u_conv, q, k, v))

        y_pool, y_conv, conv_new = _local(u_pool, u_conv, pool0[l], conv0[l], p["w_bd"],
                                          p["pool_scale"], p["conv_w"], past)

        prep = _rwkv_prep(u_rwkv, shift0[l], p["mu"], p["w0"], p["w2p"], p["a0"], p["a2p"], p["g2p"],
                          p["k_k"], p["k_a"], p["hsum"])
        prep = [_pad_time(a, SCAN_CHUNK) for a in prep]
        y_rwkv, wkv_new = _rwkv_scan(*prep, wkv0[l], p["r_k"], p["ln_w"], p["ln_b"])
        y_rwkv = y_rwkv[:, :tlen]

        if past:
            k_all = jnp.concatenate([k_past[l].reshape(bsz, past, GROUP_W), k], axis=1)
            v_all = jnp.concatenate([v_past[l].reshape(bsz, past, GROUP_W), v], axis=1)
        else:
            k_all, v_all = k, v
        y_sb = _sb_attn(q, _pad_time(k_all, ATTN_TILE), _pad_time(v_all, ATTN_TILE), past)

        flat = lambda a: a.reshape(n, GROUP_W)
        x = _outproj(x, flat(y_pool), flat(y_rwkv), flat(y_conv), flat(y_sb), p["w_out"], p["g"][3])
        x = _ffn(x, p["g"][4], p["g"][5], p["wg"][1], p["wu"][1], p["wd"][1])

        ks.append(k.reshape(bsz, tlen, N_HEADS, HEAD_DIM))
        vs.append(v.reshape(bsz, tlen, N_HEADS, HEAD_DIM))
        wkvs.append(wkv_new)
        shifts.append(u_rwkv[:, -1])
        convs.append(conv_new)
        pools.append(u_pool[:, -POOL_STATE:])
    return (x.reshape(bsz, tlen, D_MODEL), jnp.stack(ks), jnp.stack(vs), jnp.stack(wkvs),
            jnp.stack(shifts), jnp.stack(convs), jnp.stack(pools))


def kernel(x_prompt, x_sample, cache_sb_k, cache_sb_v, state_wkv, state_shift, state_conv, state_pool,
           norm_g, ffn_w_gate, ffn_w_up, ffn_w_down, w_in, w_out, pool_w, pool_scale,
           rwkv_mu, rwkv_w0, rwkv_w2, rwkv_a0, rwkv_a2, rwkv_g2, rwkv_k_k, rwkv_k_a, rwkv_r_k,
           rwkv_ln_w, rwkv_ln_b, conv_w):
    layers = _prep_params(dict(
        norm_g=norm_g, ffn_w_gate=ffn_w_gate, ffn_w_up=ffn_w_up, ffn_w_down=ffn_w_down, w_in=w_in,
        w_out=w_out, pool_w=pool_w, pool_scale=pool_scale, rwkv_mu=rwkv_mu, rwkv_w0=rwkv_w0,
        rwkv_w2=rwkv_w2, rwkv_a0=rwkv_a0, rwkv_a2=rwkv_a2, rwkv_g2=rwkv_g2, rwkv_k_k=rwkv_k_k,
        rwkv_k_a=rwkv_k_a, rwkv_r_k=rwkv_r_k, rwkv_ln_w=rwkv_ln_w, rwkv_ln_b=rwkv_ln_b, conv_w=conv_w))
    depth = w_in.shape[0]
    bsz = x_prompt.shape[0]
    z_wkv = jnp.zeros((depth, bsz, N_HEADS, HEAD_DIM, HEAD_DIM), F32)
    z_shift = jnp.zeros((depth, bsz, RWKV_IN), F32)
    z_conv = jnp.zeros((depth, bsz, CONV_W - 1, GROUP_W), F32)
    z_pool = jnp.zeros((depth, bsz, POOL_STATE, GROUP_W), F32)
    prompt = _trunk(x_prompt, None, None, z_wkv, z_shift, z_conv, z_pool, layers, 0)
    sample = _trunk(x_sample, cache_sb_k, cache_sb_v, state_wkv, state_shift, state_conv, state_pool,
                    layers, cache_sb_k.shape[2])
    return (prompt[0], sample[0]) + prompt[1:] + sample[1:]
```

```python
import functools

import jax
import jax.numpy as jnp
from jax import lax
from jax.experimental import pallas as pl
from jax.experimental.pallas import tpu as pltpu

F32 = jnp.float32
BF16 = jnp.bfloat16

D_MODEL = 1024
GROUP_W = 256
HEAD_DIM = 64
N_HEADS = 4
POOL_WINDOWS = (2, 4, 8, 16)
POOL_STATE = 15
CONV_W = 3
RWKV_IN = 896
RWKV_LOWRANK_W = 128
D_FF = 2816
RMS_EPS = 1e-6
GN_EPS = 64e-5
LOG2_E = 1.4426950408889634

FF_CHUNK = 256
TOKEN_TILE = 512
TIME_TILE = 512
SCAN_CHUNK = 64
SCAN_BLOCK = 16
SCAN_BATCH = 4
INV_PASSES = 3
STATE_PASSES = 3
ATTN_TILE = 256
ATTN_UNROLL = 4
VMEM_LIMIT = 56 * 1024 * 1024


def _cparams(sem):
    return pltpu.CompilerParams(dimension_semantics=sem, vmem_limit_bytes=VMEM_LIMIT)


def _rms(x, g):
    return x * lax.rsqrt(jnp.mean(x * x, axis=-1, keepdims=True) + RMS_EPS) * g


def _dot(a, b):
    return jnp.dot(a.astype(BF16), b.astype(BF16), preferred_element_type=F32)


def _dot_nt(a, b):
    return lax.dot_general(a, b, (((1,), (1,)), ((), ())), preferred_element_type=F32)


def _split2(a):
    hi = a.astype(BF16)
    return hi, (a - hi.astype(F32)).astype(BF16)


def _dot_split(a, ones_b):
    hi, lo = _split2(a)
    return (jnp.dot(hi, ones_b, preferred_element_type=F32)
            + jnp.dot(lo, ones_b, preferred_element_type=F32))


def _full(shape):
    return pl.BlockSpec(shape, lambda *_: (0,) * len(shape))


def _ffn_body(x_ref, gpre_ref, gpost_ref, wg_ref, wu_ref, wd_ref, o_ref, a_ref):
    x = x_ref[...]
    hb = _rms(x, gpre_ref[...]).astype(BF16)
    for j in range(D_FF // FF_CHUNK):
        sl = slice(j * FF_CHUNK, (j + 1) * FF_CHUNK)
        g = jnp.dot(hb, wg_ref[:, sl], preferred_element_type=F32)
        u = jnp.dot(hb, wu_ref[:, sl], preferred_element_type=F32)
        a_ref[:, sl] = (g * jax.nn.sigmoid(g) * u).astype(BF16)
    y = jnp.dot(a_ref[...], wd_ref[...], preferred_element_type=F32)
    o_ref[...] = x + 0.5 * _rms(y, gpost_ref[...])


def _ffn(x, g_pre, g_post, wg, wu, wd):
    n = x.shape[0]
    tm = min(TOKEN_TILE, n)
    row = pl.BlockSpec((tm, D_MODEL), lambda i: (i, 0))
    return pl.pallas_call(
        _ffn_body,
        grid=(n // tm,),
        in_specs=[row, _full((1, D_MODEL)), _full((1, D_MODEL)),
                  _full((D_MODEL, D_FF)), _full((D_MODEL, D_FF)), _full((D_FF, D_MODEL))],
        out_specs=row,
        out_shape=jax.ShapeDtypeStruct((n, D_MODEL), F32),
        scratch_shapes=[pltpu.VMEM((tm, D_FF), BF16)],
        compiler_params=_cparams(("arbitrary",)),
        name="ffn",
    )(x, g_pre, g_post, wg, wu, wd)


_PROJ_EDGES = (0, 256, 1152, 1920, 2176, 2432, 2688)


def _inproj_body(n_carried, x_ref, g_ref, w_ref, *refs):
    o_refs = refs[n_carried:]
    hb = _rms(x_ref[...], g_ref[...]).astype(BF16)
    for o_ref, lo, hi in zip(o_refs, _PROJ_EDGES[:-1], _PROJ_EDGES[1:]):
        o_ref[...] = jnp.dot(hb, w_ref[:, lo:hi], preferred_element_type=F32)


def _inproj(x, g, w_in, layer, depth, kv_rows):
    n = x.shape[0]
    tm = min(TOKEN_TILE, n)
    widths = [hi - lo for lo, hi in zip(_PROJ_EDGES[:-1], _PROJ_EDGES[1:])]
    row = lambda w: pl.BlockSpec((tm, w), lambda i: (i, 0))
    slab = pl.BlockSpec((None, tm, GROUP_W), lambda i: (layer, i, 0))
    flat = lambda w: jax.ShapeDtypeStruct((n, w), F32)
    stack = jax.ShapeDtypeStruct((depth, n, GROUP_W), F32)
    carried = [] if kv_rows is None else list(kv_rows)
    return pl.pallas_call(
        functools.partial(_inproj_body, len(carried)),
        grid=(n // tm,),
        in_specs=[row(D_MODEL), _full((1, D_MODEL)), _full((D_MODEL, _PROJ_EDGES[-1]))]
                 + [pl.BlockSpec(memory_space=pl.ANY)] * len(carried),
        out_specs=[row(w) for w in widths[:4]] + [slab, slab],
        out_shape=[flat(w) for w in widths[:4]] + [stack, stack],
        input_output_aliases={3 + i: 4 + i for i in range(len(carried))},
        compiler_params=_cparams(("arbitrary",)),
        name="inproj",
    )(x, g, w_in, *carried)


def _outproj_body(x_ref, a_ref, b_ref, c_ref, d_ref, w_ref, g_ref, o_ref):
    y = _dot(a_ref[...], w_ref[0 * GROUP_W:1 * GROUP_W, :])
    y += _dot(b_ref[...], w_ref[1 * GROUP_W:2 * GROUP_W, :])
    y += _dot(c_ref[...], w_ref[2 * GROUP_W:3 * GROUP_W, :])
    y += _dot(d_ref[...], w_ref[3 * GROUP_W:4 * GROUP_W, :])
    o_ref[...] = x_ref[...] + _rms(y, g_ref[...])


def _outproj(x, ya, yb, yc, yd, w_out, g):
    n = x.shape[0]
    tm = min(TOKEN_TILE, n)
    row = pl.BlockSpec((tm, D_MODEL), lambda i: (i, 0))
    grp = pl.BlockSpec((tm, GROUP_W), lambda i: (i, 0))
    return pl.pallas_call(
        _outproj_body,
        grid=(n // tm,),
        in_specs=[row, grp, grp, grp, grp, _full((D_MODEL, D_MODEL)), _full((1, D_MODEL))],
        out_specs=row,
        out_shape=jax.ShapeDtypeStruct((n, D_MODEL), F32),
        compiler_params=_cparams(("arbitrary",)),
        name="outproj",
    )(x, ya, yb, yc, yd, w_out, g)


_POOL_PAD = 16
_ROW_PAD = 8


def _local_body(past, up_ref, uc_ref, pst_ref, cst_ref, wbd_ref, scale_ref, cw_ref,
                yp_ref, yc_ref, cso_ref, pprev_ref, cprev_ref):
    t = pl.program_id(1)
    tm = up_ref.shape[1]

    @pl.when(t == 0)
    def _():
        pprev_ref[...] = pst_ref[0]
        cprev_ref[...] = cst_ref[0]

    u = up_ref[0]
    ext = jnp.concatenate([pprev_ref[...], u], axis=0)
    s2 = ext + pltpu.roll(ext, 1, 0)
    s4 = s2 + pltpu.roll(s2, 2, 0)
    s8 = s4 + pltpu.roll(s4, 4, 0)
    s16 = s8 + pltpu.roll(s8, 8, 0)
    grp = lax.broadcasted_iota(jnp.int32, (tm, GROUP_W), 1) // (GROUP_W // len(POOL_WINDOWS))
    pos1 = past + t * tm + lax.broadcasted_iota(jnp.int32, (tm, GROUP_W), 0) + 1
    win = jnp.where(grp == 0, s2[_POOL_PAD:], jnp.where(grp == 1, s4[_POOL_PAD:],
                    jnp.where(grp == 2, s8[_POOL_PAD:], s16[_POOL_PAD:])))
    width = jnp.where(grp == 0, 2, jnp.where(grp == 1, 4, jnp.where(grp == 2, 8, 16)))
    cnt = jnp.minimum(width, pos1).astype(F32)
    d = win / cnt - u
    yp_ref[0] = _dot(d, wbd_ref[...]) * scale_ref[...]
    pprev_ref[...] = ext[tm:]

    uc = uc_ref[0]
    b = uc[:, 0:GROUP_W]
    z = uc[:, GROUP_W:2 * GROUP_W] * uc[:, 2 * GROUP_W:3 * GROUP_W]
    extz = jnp.concatenate([cprev_ref[...], z], axis=0)
    y = (pltpu.roll(extz, 2, 0)[_ROW_PAD:] * cw_ref[0:1, :]
         + pltpu.roll(extz, 1, 0)[_ROW_PAD:] * cw_ref[1:2, :]
         + z * cw_ref[2:3, :])
    yc_ref[0] = b * y
    cprev_ref[...] = extz[tm:]
    cso_ref[0] = extz[tm:][_ROW_PAD - (CONV_W - 1):]


def _local(u_pool, u_conv, pool_state, conv_state, w_bd, scale, conv_w, past):
    bsz, tlen, _ = u_pool.shape
    tm = min(TIME_TILE, tlen)
    pst = jnp.pad(pool_state, ((0, 0), (_POOL_PAD - POOL_STATE, 0), (0, 0)))
    cst = jnp.pad(conv_state, ((0, 0), (_ROW_PAD - (CONV_W - 1), 0), (0, 0)))
    return pl.pallas_call(
        functools.partial(_local_body, past),
        grid=(bsz, tlen // tm),
        in_specs=[pl.BlockSpec((1, tm, GROUP_W), lambda b, t: (b, t, 0)),
                  pl.BlockSpec((1, tm, 3 * GROUP_W), lambda b, t: (b, t, 0)),
                  pl.BlockSpec((1, _POOL_PAD, GROUP_W), lambda b, t: (b, 0, 0)),
                  pl.BlockSpec((1, _ROW_PAD, GROUP_W), lambda b, t: (b, 0, 0)),
                  _full((GROUP_W, GROUP_W)), _full((1, GROUP_W)), _full((CONV_W, GROUP_W))],
        out_specs=[pl.BlockSpec((1, tm, GROUP_W), lambda b, t: (b, t, 0)),
                   pl.BlockSpec((1, tm, GROUP_W), lambda b, t: (b, t, 0)),
                   pl.BlockSpec((1, CONV_W - 1, GROUP_W), lambda b, t: (b, 0, 0))],
        out_shape=[jax.ShapeDtypeStruct((bsz, tlen, GROUP_W), F32),
                   jax.ShapeDtypeStruct((bsz, tlen, GROUP_W), F32),
                   jax.ShapeDtypeStruct((bsz, CONV_W - 1, GROUP_W), F32)],
        scratch_shapes=[pltpu.VMEM((_POOL_PAD, GROUP_W), F32), pltpu.VMEM((_ROW_PAD, GROUP_W), F32)],
        compiler_params=_cparams(("arbitrary", "arbitrary")),
        name="local_mixers",
    )(u_pool, u_conv, pst, cst, w_bd, scale, conv_w)


def _softplus(x):
    return jnp.maximum(x, 0.0) + jnp.log1p(jnp.exp(-jnp.abs(x)))


_NN = ((2,), (1,))
_NT = ((2,), (2,))
_TN = ((1,), (1,))


def _bmm(a, b, dims=_NN, passes=1):
    dg = lambda x, y: lax.dot_general(x, y, (dims, ((0,), (0,))), preferred_element_type=F32)
    if passes == 1:
        return dg(a.astype(BF16), b.astype(BF16))
    ah, al = _split2(a)
    bh, bl = _split2(b)
    return dg(ah, bh) + (dg(ah, bl) + dg(al, bh))


def _unit_lower_inverse(a, c):
    row = lax.broadcasted_iota(jnp.int32, (c, c), 0)
    col = lax.broadcasted_iota(jnp.int32, (c, c), 1)
    eye = (row == col).astype(F32)

    def series(m, order):
        x = eye - m
        p = m
        n = 2
        while n < order:
            p = _bmm(p, p, passes=INV_PASSES)
            x = x + _bmm(x, p, passes=INV_PASSES)
            n *= 2
        return x

    if c == SCAN_BLOCK:
        return series(a, c)
    same_block = (row // SCAN_BLOCK) == (col // SCAN_BLOCK)
    a_diag = jnp.where(same_block, a, 0.0)
    t_diag = series(a_diag, SCAN_BLOCK)
    b = _bmm(t_diag, a - a_diag, passes=INV_PASSES)
    return _bmm(series(b, c // SCAN_BLOCK), t_diag, passes=INV_PASSES)


def _scan_body(valid, u_ref, sh_ref, mu_ref, w0_ref, w2_ref, a0_ref, a2_ref, g2_ref, kk_ref, ka_ref, hsum_ref,
               s0_ref, rk_ref, lnw_ref, lnb_ref, y_ref, so_ref, s_ref, prev_ref):
    t = pl.program_id(1)
    nb, c, _ = u_ref.shape

    @pl.when(t == 0)
    def _():
        s_ref[...] = s0_ref[...]
        prev_ref[...] = sh_ref[...]

    shifted = []
    for i in range(nb):
        u = u_ref[i]
        ext = jnp.concatenate([prev_ref[i], u], axis=0)
        prev_ref[i] = ext[c:]
        shifted.append(u + (pltpu.roll(ext, 1, 0)[_ROW_PAD:] - u) * mu_ref[...])
    xs = jnp.concatenate(shifted, axis=0)
    low = xs[:, 3 * GROUP_W:]
    w = -_softplus(-(w0_ref[...] + _dot(jnp.tanh(low), w2_ref[...]))) - 0.5
    a = jax.nn.sigmoid(a0_ref[...] + _dot(low, a2_ref[...]))
    k_raw = xs[:, GROUP_W:2 * GROUP_W]
    kk = k_raw * kk_ref[...]
    norm = jnp.sqrt(_dot_split(kk * kk, hsum_ref[...]))
    kap = kk / jnp.maximum(norm, 1e-12)
    steps = [xs[:, 0:GROUP_W],
             -jnp.exp(w),
             k_raw * (1.0 + (a - 1.0) * ka_ref[...]),
             xs[:, 2 * GROUP_W:3 * GROUP_W],
             kap,
             kap * a]
    if valid < c:
        live = lax.broadcasted_iota(jnp.int32, (nb * c, GROUP_W), 0) % c < valid
        steps = [jnp.where(live, x, 0.0) for x in steps]
    r_all, lw, k_all, v_all, kap_all, b_all = (x.reshape(nb, c, GROUP_W) for x in steps)
    gate = _dot(jax.nn.sigmoid(low), g2_ref[...]).reshape(nb, c, GROUP_W)

    row = lax.broadcasted_iota(jnp.int32, (c, c), 0)
    col = lax.broadcasted_iota(jnp.int32, (c, c), 1)
    incl = row >= col
    strict = row > col
    tri = incl.astype(BF16)

    def heads(x):
        return jnp.stack([x[i][:, h * HEAD_DIM:(h + 1) * HEAD_DIM]
                          for i in range(nb) for h in range(N_HEADS)])

    lw_hi, lw_lo = _split2(lw)
    lw_lo2 = (lw - lw_hi.astype(F32) - lw_lo.astype(F32)).astype(BF16)
    cum = jnp.stack([jnp.dot(tri, lw_hi[i], preferred_element_type=F32)
                     + (jnp.dot(tri, lw_lo[i], preferred_element_type=F32)
                        + jnp.dot(tri, lw_lo2[i], preferred_element_type=F32)) for i in range(nb)])
    cmid = cum[:, c // 2 - 1:c // 2, :]
    cend = cum[:, c - 1:c, :]
    e_bwd = jnp.exp(cmid - cum)
    e_end = jnp.exp(cend - cum)
    r_t = heads(r_all * jnp.exp(cum - cmid))
    kap_t = heads(kap_all * jnp.exp(cum - lw - cmid))
    k_t = heads(k_all * e_bwd)
    b_t = heads(b_all * e_bwd)
    r_0 = heads(r_all * jnp.exp(cum))
    kap_0 = heads(kap_all * jnp.exp(cum - lw))
    k_e = heads(k_all * e_end)
    b_e = heads(b_all * e_end)
    g_end = heads(jnp.exp(cend))
    v = heads(v_all)
    s = s_ref[...].reshape(nb * N_HEADS, HEAD_DIM, HEAD_DIM)

    a_kb = jnp.where(strict, _bmm(kap_t, b_t, _NT), 0.0)
    a_kk = jnp.where(strict, _bmm(kap_t, k_t, _NT), 0.0)
    a_rb = jnp.where(incl, _bmm(r_t, b_t, _NT), 0.0)
    a_rk = jnp.where(incl, _bmm(r_t, k_t, _NT), 0.0)
    t_inv = _unit_lower_inverse(a_kb, c)

    u = _bmm(t_inv, _bmm(kap_0, s, _NT, STATE_PASSES) + _bmm(a_kk, v), passes=STATE_PASSES)
    y = _bmm(r_0, s, _NT, STATE_PASSES) + _bmm(a_rk, v) - _bmm(a_rb, u)
    s_new = s * g_end + _bmm(v, k_e, _TN, STATE_PASSES) - _bmm(u, b_e, _TN, STATE_PASSES)
    s_new = s_new.reshape(nb, N_HEADS, HEAD_DIM, HEAD_DIM)
    s_ref[...] = s_new
    so_ref[...] = s_new

    mean = jnp.mean(y, axis=-1, keepdims=True)
    yc = y - mean
    var = jnp.mean(yc * yc, axis=-1, keepdims=True)
    yn = yc * lax.rsqrt(var + GN_EPS)
    bonus = jnp.sum(heads(r_all * k_all * rk_ref[...]), axis=-1, keepdims=True) * v
    for i in range(nb):
        wide = lambda x: jnp.concatenate([x[i * N_HEADS + h] for h in range(N_HEADS)], axis=-1)
        y_ref[i] = (wide(yn) * lnw_ref[...] + lnb_ref[...] + wide(bonus)) * gate[i]


def _rwkv_mixer(u, shift_state, s0, p):
    bsz, tlen, _ = u.shape
    c = SCAN_CHUNK
    nb = SCAN_BATCH
    u = _pad_time(u, c)
    sh = jnp.pad(shift_state[:, None, :], ((0, 0), (_ROW_PAD - 1, 0), (0, 0)))
    st = pl.BlockSpec((nb, N_HEADS, HEAD_DIM, HEAD_DIM), lambda b_, t: (b_, 0, 0, 0))
    vec = _full((1, GROUP_W))
    low = _full((RWKV_LOWRANK_W, GROUP_W))
    y, state = pl.pallas_call(
        functools.partial(_scan_body, min(tlen, c)),
        grid=(bsz // nb, u.shape[1] // c),
        in_specs=[pl.BlockSpec((nb, c, RWKV_IN), lambda b_, t: (b_, t, 0)),
                  pl.BlockSpec((nb, _ROW_PAD, RWKV_IN), lambda b_, t: (b_, 0, 0)),
                  _full((1, RWKV_IN)), vec, low, vec, low, low, vec, vec, _full((GROUP_W, GROUP_W)),
                  st, vec, vec, vec],
        out_specs=[pl.BlockSpec((nb, c, GROUP_W), lambda b_, t: (b_, t, 0)), st],
        out_shape=[jax.ShapeDtypeStruct((bsz, u.shape[1], GROUP_W), F32),
                   jax.ShapeDtypeStruct((bsz, N_HEADS, HEAD_DIM, HEAD_DIM), F32)],
        scratch_shapes=[pltpu.VMEM((nb, N_HEADS, HEAD_DIM, HEAD_DIM), F32),
                        pltpu.VMEM((nb, _ROW_PAD, RWKV_IN), F32)],
        compiler_params=_cparams(("arbitrary", "arbitrary")),
        name="rwkv_mixer",
    )(u, sh, p["mu"], p["w0"], p["w2p"], p["a0"], p["a2p"], p["g2p"], p["k_k"], p["k_a"], p["hsum"],
      s0, p["r_k"], p["ln_w"], p["ln_b"])
    return y[:, :tlen], state


def _attn_body(past, q_ref, k_ref, v_ref, o_ref, kb_ref, vb_ref):
    qi = pl.program_id(1)
    tq = q_ref.shape[1]
    tk = ATTN_TILE
    q0 = past + qi * tq
    diag = q0 // tk
    scale = HEAD_DIM ** -0.5 * LOG2_E

    @pl.when(qi == 0)
    def _():
        kb_ref[...] = k_ref[0].astype(BF16)
        vb_ref[...] = v_ref[0].astype(BF16)

    heads = [slice(h * HEAD_DIM, (h + 1) * HEAD_DIM) for h in range(N_HEADS)]
    qs = [(q_ref[0, :, sl] * scale).astype(BF16) for sl in heads]
    later = (lax.broadcasted_iota(jnp.int32, (tk, tk), 0)
             > lax.broadcasted_iota(jnp.int32, (tk, tk), 1)).astype(BF16)
    every = range(N_HEADS)

    def tiles(js, carry, masked):
        accs, runs = carry
        units = [(t, h) for t in range(len(js)) for h in every]
        kt = [kb_ref[pl.ds(pl.multiple_of(j * tk, tk), tk), :] for j in js]
        vt = [vb_ref[pl.ds(pl.multiple_of(j * tk, tk), tk), :] for j in js]
        if masked:
            vis = [(j * tk + lax.broadcasted_iota(jnp.int32, (tq, tk), 1)
                    < q0 + lax.broadcasted_iota(jnp.int32, (tq, tk), 0)) for j in js]
        z = [_dot_nt(qs[h], kt[t][:, heads[h]]) for t, h in units]
        log_take = [jnp.minimum(x, 0.0) - jnp.log2(1.0 + jnp.exp2(-jnp.abs(x))) for x in z]
        log_keep = [a - b for a, b in zip(log_take, z)]
        if masked:
            log_keep = [jnp.where(vis[t], x, 0.0) for (t, _), x in zip(units, log_keep)]
        after = [_dot(x, later) for x in log_keep]
        p = [jnp.exp2(a + b) for a, b in zip(log_take, after)]
        if masked:
            p = [jnp.where(vis[t], x, 0.0) for (t, _), x in zip(units, p)]
        part = [jnp.dot(x.astype(BF16), vt[t][:, heads[h]], preferred_element_type=F32)
                for (t, h), x in zip(units, p)]
        accs, runs = list(accs), list(runs)
        for n, (_, h) in enumerate(units):
            accs[h] = accs[h] + jnp.exp2(runs[h]) * part[n]
            runs[h] = runs[h] + (after[n][:, 0:1] + log_keep[n][:, 0:1])
        return tuple(accs), tuple(runs)

    carry = (tuple(jnp.zeros((tq, HEAD_DIM), F32) for _ in every),
             tuple(jnp.zeros((tq, 1), F32) for _ in every))
    carry = tiles([diag], carry, True)
    single = diag % ATTN_UNROLL
    carry = lax.fori_loop(0, single, lambda i, c: tiles([diag - 1 - i], c, False), carry)
    first = diag - single - 1
    accs, _ = lax.fori_loop(
        0, diag // ATTN_UNROLL,
        lambda i, c: tiles([first - ATTN_UNROLL * i - t for t in range(ATTN_UNROLL)], c, False), carry)
    o_ref[0] = jnp.concatenate(accs, axis=-1)


def _sb_attn(q, k_all, v_all, layer, past):
    bsz, tlen, _ = q.shape
    tq = min(ATTN_TILE, tlen)
    slen = k_all.shape[2]
    kv = pl.BlockSpec((None, 1, slen, GROUP_W), lambda b, i: (layer, b, 0, 0))
    return pl.pallas_call(
        functools.partial(_attn_body, past),
        grid=(bsz, tlen // tq),
        in_specs=[pl.BlockSpec((1, tq, GROUP_W), lambda b, i: (b, i, 0)), kv, kv],
        out_specs=pl.BlockSpec((1, tq, GROUP_W), lambda b, i: (b, i, 0)),
        out_shape=jax.ShapeDtypeStruct((bsz, tlen, GROUP_W), F32),
        scratch_shapes=[pltpu.VMEM((slen, GROUP_W), BF16), pltpu.VMEM((slen, GROUP_W), BF16)],
        compiler_params=_cparams(("arbitrary", "arbitrary")),
        name="sb_attn",
    )(q, k_all, v_all)


def _pad_time(x, mult):
    pad = (-x.shape[1]) % mult
    return x if pad == 0 else jnp.pad(x, ((0, 0), (0, pad), (0, 0)))


def _prep_params(prm):
    out = []
    depth = prm["w_in"].shape[0]
    head_of = jnp.arange(GROUP_W) // HEAD_DIM
    hsum = (head_of[:, None] == head_of[None, :]).astype(BF16)
    row2 = lambda v: v.reshape(1, -1)
    for l in range(depth):
        w_bd = jax.scipy.linalg.block_diag(*[prm["pool_w"][l, i] for i in range(len(POOL_WINDOWS))])
        lowpad = lambda w, lo: jnp.zeros((RWKV_LOWRANK_W, GROUP_W), F32).at[lo:lo + w.shape[0]].set(w)
        out.append(dict(
            g=[row2(prm["norm_g"][l, i]) for i in range(6)],
            wg=[prm["ffn_w_gate"][l, i].astype(BF16) for i in range(2)],
            wu=[prm["ffn_w_up"][l, i].astype(BF16) for i in range(2)],
            wd=[prm["ffn_w_down"][l, i].astype(BF16) for i in range(2)],
            w_in=prm["w_in"][l].astype(BF16),
            w_out=prm["w_out"][l].astype(BF16),
            w_bd=w_bd.astype(BF16),
            pool_scale=row2(prm["pool_scale"][l]),
            conv_w=prm["conv_w"][l],
            mu=row2(prm["rwkv_mu"][l]),
            w0=row2(prm["rwkv_w0"][l]),
            w2p=lowpad(prm["rwkv_w2"][l], 0).astype(BF16),
            a0=row2(prm["rwkv_a0"][l]),
            a2p=lowpad(prm["rwkv_a2"][l], 32).astype(BF16),
            g2p=lowpad(prm["rwkv_g2"][l], 64).astype(BF16),
            k_k=row2(prm["rwkv_k_k"][l]),
            k_a=row2(prm["rwkv_k_a"][l]),
            r_k=row2(prm["rwkv_r_k"][l]),
            ln_w=row2(prm["rwkv_ln_w"][l]),
            ln_b=row2(prm["rwkv_ln_b"][l]),
            hsum=hsum,
        ))
    return out


def _trunk(x, k_past, v_past, wkv0, shift0, conv0, pool0, layers, past):
    bsz, tlen, _ = x.shape
    n = bsz * tlen
    depth = len(layers)
    x = x.reshape(n, D_MODEL)
    kv_rows = None
    wkvs, shifts, convs, pools = [], [], [], []
    for l, p in enumerate(layers):
        x = _ffn(x, p["g"][0], p["g"][1], p["wg"][0], p["wu"][0], p["wd"][0])
        u_pool, u_rwkv, u_conv, q, *kv_rows = _inproj(x, p["g"][2], p["w_in"], l, depth, kv_rows)
        seq = lambda a: a.reshape(bsz, tlen, -1)
        u_pool, u_rwkv, u_conv, q = map(seq, (u_pool, u_rwkv, u_conv, q))
        k_rows, v_rows = (a.reshape(depth, bsz, tlen, GROUP_W) for a in kv_rows)

        y_pool, y_conv, conv_new = _local(u_pool, u_conv, pool0[l], conv0[l], p["w_bd"],
                                          p["pool_scale"], p["conv_w"], past)

        y_rwkv, wkv_new = _rwkv_mixer(u_rwkv, shift0[l], wkv0[l], p)

        if past:
            with_past = lambda old, new: _pad_time(
                jnp.concatenate([old[l].reshape(bsz, past, GROUP_W), new[l]], axis=1), ATTN_TILE)[None]
            y_sb = _sb_attn(q, with_past(k_past, k_rows), with_past(v_past, v_rows), 0, past)
        else:
            y_sb = _sb_attn(q, k_rows, v_rows, l, past)

        flat = lambda a: a.reshape(n, GROUP_W)
        x = _outproj(x, flat(y_pool), flat(y_rwkv), flat(y_conv), flat(y_sb), p["w_out"], p["g"][3])
        x = _ffn(x, p["g"][4], p["g"][5], p["wg"][1], p["wu"][1], p["wd"][1])

        wkvs.append(wkv_new)
        shifts.append(u_rwkv[:, -1])
        convs.append(conv_new)
        pools.append(u_pool[:, -POOL_STATE:])
    per_head = lambda a: a.reshape(depth, bsz, tlen, N_HEADS, HEAD_DIM)
    return (x.reshape(bsz, tlen, D_MODEL), per_head(kv_rows[0]), per_head(kv_rows[1]), jnp.stack(wkvs),
            jnp.stack(shifts), jnp.stack(convs), jnp.stack(pools))


def kernel(x_prompt, x_sample, cache_sb_k, cache_sb_v, state_wkv, state_shift, state_conv, state_pool,
           norm_g, ffn_w_gate, ffn_w_up, ffn_w_down, w_in, w_out, pool_w, pool_scale,
           rwkv_mu, rwkv_w0, rwkv_w2, rwkv_a0, rwkv_a2, rwkv_g2, rwkv_k_k, rwkv_k_a, rwkv_r_k,
           rwkv_ln_w, rwkv_ln_b, conv_w):
    layers = _prep_params(dict(
        norm_g=norm_g, ffn_w_gate=ffn_w_gate, ffn_w_up=ffn_w_up, ffn_w_down=ffn_w_down, w_in=w_in,
        w_out=w_out, pool_w=pool_w, pool_scale=pool_scale, rwkv_mu=rwkv_mu, rwkv_w0=rwkv_w0,
        rwkv_w2=rwkv_w2, rwkv_a0=rwkv_a0, rwkv_a2=rwkv_a2, rwkv_g2=rwkv_g2, rwkv_k_k=rwkv_k_k,
        rwkv_k_a=rwkv_k_a, rwkv_r_k=rwkv_r_k, rwkv_ln_w=rwkv_ln_w, rwkv_ln_b=rwkv_ln_b, conv_w=conv_w))
    depth = w_in.shape[0]
    bsz = x_prompt.shape[0]
    z_wkv = jnp.zeros((depth, bsz, N_HEADS, HEAD_DIM, HEAD_DIM), F32)
    z_shift = jnp.zeros((depth, bsz, RWKV_IN), F32)
    z_conv = jnp.zeros((depth, bsz, CONV_W - 1, GROUP_W), F32)
    z_pool = jnp.zeros((depth, bsz, POOL_STATE, GROUP_W), F32)
    prompt = _trunk(x_prompt, None, None, z_wkv, z_shift, z_conv, z_pool, layers, 0)
    sample = _trunk(x_sample, cache_sb_k, cache_sb_v, state_wkv, state_shift, state_conv, state_pool,
                    layers, cache_sb_k.shape[2])
    return (prompt[0], sample[0]) + prompt[1:] + sample[1:]
```

```python
import functools

import jax
import jax.numpy as jnp
from jax import lax
from jax.experimental import pallas as pl
from jax.experimental.pallas import tpu as pltpu

F32 = jnp.float32
BF16 = jnp.bfloat16

D_MODEL = 1024
GROUP_W = 256
HEAD_DIM = 64
N_HEADS = 4
POOL_WINDOWS = (2, 4, 8, 16)
POOL_STATE = 15
CONV_W = 3
RWKV_IN = 896
RWKV_LOWRANK_W = 128
D_FF = 2816
RMS_EPS = 1e-6
GN_EPS = 64e-5
LOG2_E = 1.4426950408889634

FF_CHUNK = 256
TOKEN_TILE = 512
TIME_TILE = 512
SCAN_CHUNK = 64
SCAN_BLOCK = 16
PACK_W = 256
SCAN_BATCH = 8
INV_PASSES = 3
STATE_PASSES = 3
ATTN_TILE = 256
ATTN_UNROLL = 4
VMEM_LIMIT = 56 * 1024 * 1024


def _cparams(sem):
    return pltpu.CompilerParams(dimension_semantics=sem, vmem_limit_bytes=VMEM_LIMIT)


def _rms(x, g):
    return x * lax.rsqrt(jnp.mean(x * x, axis=-1, keepdims=True) + RMS_EPS) * g


def _dot(a, b):
    return jnp.dot(a.astype(BF16), b.astype(BF16), preferred_element_type=F32)


def _dot_nt(a, b):
    return lax.dot_general(a, b, (((1,), (1,)), ((), ())), preferred_element_type=F32)


def _split2(a):
    hi = a.astype(BF16)
    return hi, (a - hi.astype(F32)).astype(BF16)


def _dot_split(a, ones_b):
    hi, lo = _split2(a)
    return (jnp.dot(hi, ones_b, preferred_element_type=F32)
            + jnp.dot(lo, ones_b, preferred_element_type=F32))


def _full(shape):
    return pl.BlockSpec(shape, lambda *_: (0,) * len(shape))


def _resident(shape):
    return pl.BlockSpec(shape, lambda *_: (0,) * len(shape), pipeline_mode=pl.Buffered(1))


def _ffn_body(mixed, x_ref, *refs):
    if mixed:
        mix_refs, wo_ref, gmix_ref = refs[:4], refs[4], refs[5]
        refs = refs[6:]
        y = sum(jnp.dot(m_ref[...], wo_ref[i * GROUP_W:(i + 1) * GROUP_W, :], preferred_element_type=F32)
                for i, m_ref in enumerate(mix_refs))
        x = x_ref[...] + _rms(y, gmix_ref[...])
    else:
        x = x_ref[...]
    gpre_ref, gpost_ref, wg_ref, wu_ref, wd_ref, o_ref, a_ref = refs
    hb = _rms(x, gpre_ref[...]).astype(BF16)
    for j in range(D_FF // FF_CHUNK):
        sl = slice(j * FF_CHUNK, (j + 1) * FF_CHUNK)
        g = jnp.dot(hb, wg_ref[:, sl], preferred_element_type=F32)
        u = jnp.dot(hb, wu_ref[:, sl], preferred_element_type=F32)
        a_ref[:, sl] = (g * jax.nn.sigmoid(g) * u).astype(BF16)
    y = jnp.dot(a_ref[...], wd_ref[...], preferred_element_type=F32)
    o_ref[...] = x + 0.5 * _rms(y, gpost_ref[...])


def _ffn(x, g_pre, g_post, wg, wu, wd, mix=None):
    n = x.shape[0]
    tm = min(TOKEN_TILE, n)
    row = pl.BlockSpec((tm, D_MODEL), lambda i: (i, 0))
    grp = pl.BlockSpec((tm, GROUP_W), lambda i: (i, 0))
    mix_specs = [] if mix is None else [grp] * 4 + [_resident((D_MODEL, D_MODEL)), _full((1, D_MODEL))]
    return pl.pallas_call(
        functools.partial(_ffn_body, mix is not None),
        grid=(n // tm,),
        in_specs=[row] + mix_specs + [_full((1, D_MODEL)), _full((1, D_MODEL)), _resident((D_MODEL, D_FF)),
                                      _resident((D_MODEL, D_FF)), _resident((D_FF, D_MODEL))],
        out_specs=row,
        out_shape=jax.ShapeDtypeStruct((n, D_MODEL), F32),
        scratch_shapes=[pltpu.VMEM((tm, D_FF), BF16)],
        compiler_params=_cparams(("arbitrary",)),
        name="ffn",
    )(x, *([] if mix is None else mix), g_pre, g_post, wg, wu, wd)


_PROJ_EDGES = (0, 256, 1152, 1920, 2176, 2432, 2688)


def _inproj_body(x_ref, g_ref, w_ref, *o_refs):
    hb = _rms(x_ref[...], g_ref[...]).astype(BF16)
    for o_ref, lo, hi in zip(o_refs, _PROJ_EDGES[:-1], _PROJ_EDGES[1:]):
        o_ref[...] = jnp.dot(hb, w_ref[:, lo:hi], preferred_element_type=F32)


def _inproj(x, g, w_in):
    n = x.shape[0]
    tm = min(TOKEN_TILE, n)
    widths = [hi - lo for lo, hi in zip(_PROJ_EDGES[:-1], _PROJ_EDGES[1:])]
    return pl.pallas_call(
        _inproj_body,
        grid=(n // tm,),
        in_specs=[pl.BlockSpec((tm, D_MODEL), lambda i: (i, 0)), _full((1, D_MODEL)),
                  _resident((D_MODEL, _PROJ_EDGES[-1]))],
        out_specs=[pl.BlockSpec((tm, w), lambda i: (i, 0)) for w in widths],
        out_shape=[jax.ShapeDtypeStruct((n, w), F32) for w in widths],
        compiler_params=_cparams(("arbitrary",)),
        name="inproj",
    )(x, g, w_in)


_POOL_PAD = 16
_ROW_PAD = 8


def _local_body(past, up_ref, uc_ref, pst_ref, cst_ref, wbd_ref, scale_ref, cw_ref,
                yp_ref, yc_ref, cso_ref, pprev_ref, cprev_ref):
    t = pl.program_id(1)
    tm = up_ref.shape[1]

    @pl.when(t == 0)
    def _():
        pprev_ref[...] = pst_ref[0]
        cprev_ref[...] = cst_ref[0]

    u = up_ref[0]
    ext = jnp.concatenate([pprev_ref[...], u], axis=0)
    s2 = ext + pltpu.roll(ext, 1, 0)
    s4 = s2 + pltpu.roll(s2, 2, 0)
    s8 = s4 + pltpu.roll(s4, 4, 0)
    s16 = s8 + pltpu.roll(s8, 8, 0)
    grp = lax.broadcasted_iota(jnp.int32, (tm, GROUP_W), 1) // (GROUP_W // len(POOL_WINDOWS))
    pos1 = past + t * tm + lax.broadcasted_iota(jnp.int32, (tm, GROUP_W), 0) + 1
    win = jnp.where(grp == 0, s2[_POOL_PAD:], jnp.where(grp == 1, s4[_POOL_PAD:],
                    jnp.where(grp == 2, s8[_POOL_PAD:], s16[_POOL_PAD:])))
    width = jnp.where(grp == 0, 2, jnp.where(grp == 1, 4, jnp.where(grp == 2, 8, 16)))
    cnt = jnp.minimum(width, pos1).astype(F32)
    d = win / cnt - u
    yp_ref[0] = (_dot(d, wbd_ref[...]) * scale_ref[...]).astype(BF16)
    pprev_ref[...] = ext[tm:]

    uc = uc_ref[0]
    b = uc[:, 0:GROUP_W]
    z = uc[:, GROUP_W:2 * GROUP_W] * uc[:, 2 * GROUP_W:3 * GROUP_W]
    extz = jnp.concatenate([cprev_ref[...], z], axis=0)
    y = (pltpu.roll(extz, 2, 0)[_ROW_PAD:] * cw_ref[0:1, :]
         + pltpu.roll(extz, 1, 0)[_ROW_PAD:] * cw_ref[1:2, :]
         + z * cw_ref[2:3, :])
    yc_ref[0] = (b * y).astype(BF16)
    cprev_ref[...] = extz[tm:]
    cso_ref[0] = extz[tm:][_ROW_PAD - (CONV_W - 1):]


def _local(u_pool, u_conv, pool_state, conv_state, w_bd, scale, conv_w, past):
    bsz, tlen, _ = u_pool.shape
    tm = min(TIME_TILE, tlen)
    pst = jnp.pad(pool_state, ((0, 0), (_POOL_PAD - POOL_STATE, 0), (0, 0)))
    cst = jnp.pad(conv_state, ((0, 0), (_ROW_PAD - (CONV_W - 1), 0), (0, 0)))
    return pl.pallas_call(
        functools.partial(_local_body, past),
        grid=(bsz, tlen // tm),
        in_specs=[pl.BlockSpec((1, tm, GROUP_W), lambda b, t: (b, t, 0)),
                  pl.BlockSpec((1, tm, 3 * GROUP_W), lambda b, t: (b, t, 0)),
                  pl.BlockSpec((1, _POOL_PAD, GROUP_W), lambda b, t: (b, 0, 0)),
                  pl.BlockSpec((1, _ROW_PAD, GROUP_W), lambda b, t: (b, 0, 0)),
                  _full((GROUP_W, GROUP_W)), _full((1, GROUP_W)), _full((CONV_W, GROUP_W))],
        out_specs=[pl.BlockSpec((1, tm, GROUP_W), lambda b, t: (b, t, 0)),
                   pl.BlockSpec((1, tm, GROUP_W), lambda b, t: (b, t, 0)),
                   pl.BlockSpec((1, CONV_W - 1, GROUP_W), lambda b, t: (b, 0, 0))],
        out_shape=[jax.ShapeDtypeStruct((bsz, tlen, GROUP_W), BF16),
                   jax.ShapeDtypeStruct((bsz, tlen, GROUP_W), BF16),
                   jax.ShapeDtypeStruct((bsz, CONV_W - 1, GROUP_W), F32)],
        scratch_shapes=[pltpu.VMEM((_POOL_PAD, GROUP_W), F32), pltpu.VMEM((_ROW_PAD, GROUP_W), F32)],
        compiler_params=_cparams(("arbitrary", "arbitrary")),
        name="local_mixers",
    )(u_pool, u_conv, pst, cst, w_bd, scale, conv_w)


def _softplus(x):
    return jnp.maximum(x, 0.0) + jnp.log1p(jnp.exp(-jnp.abs(x)))


_NN = ((2,), (1,))
_NT = ((2,), (2,))
_TN = ((1,), (1,))


def _bmm(a, b, dims=_NN, passes=1):
    dg = lambda x, y: lax.dot_general(x, y, (dims, ((0,), (0,))), preferred_element_type=F32)
    if passes == 1:
        return dg(a.astype(BF16), b.astype(BF16))
    ah, al = _split2(a)
    bh, bl = _split2(b)
    if dims == _TN:
        return dg(ah, bh) + (dg(ah, bl) + dg(al, bh))
    m = a.shape[1]
    both = dg(jnp.concatenate([ah, al], axis=1), bh)
    return both[:, :m] + (both[:, m:] + dg(ah, bl))


def _series_inverse(m, eye, order, mm):
    x = eye - m
    p = m
    n = 2
    while n < order:
        p = mm(p, p)
        x = x + mm(x, p)
        n *= 2
    return x


def _diag_block_inverse(a_diag, same_block, c):
    n = a_diag.shape[0]
    per_row = PACK_W // c
    folded = sum(a_diag[:, i * SCAN_BLOCK:(i + 1) * SCAN_BLOCK, :] for i in range(c // SCAN_BLOCK))
    packed = jnp.stack([jnp.concatenate([folded[g * per_row + j] for j in range(per_row)], axis=-1)
                        for g in range(n // per_row)])
    lane = lax.broadcasted_iota(jnp.int32, (SCAN_BLOCK, PACK_W), 1)
    eye = (lax.broadcasted_iota(jnp.int32, (SCAN_BLOCK, PACK_W), 0) == lane % SCAN_BLOCK).astype(F32)
    on_diag = ((lax.broadcasted_iota(jnp.int32, (PACK_W, PACK_W), 0) // SCAN_BLOCK)
               == (lax.broadcasted_iota(jnp.int32, (PACK_W, PACK_W), 1) // SCAN_BLOCK)).astype(BF16)

    def spread(y):
        return jnp.concatenate([y] * (PACK_W // SCAN_BLOCK), axis=1) * on_diag

    def mm(x, y):
        dg = lambda p, q: lax.dot_general(p, q, (_NN, ((0,), (0,))), preferred_element_type=F32)
        if INV_PASSES == 1:
            return dg(x.astype(BF16), spread(y.astype(BF16)))
        xh, xl = _split2(x)
        yh, yl = _split2(y)
        yh_d = spread(yh)
        return dg(xh, yh_d) + (dg(xh, spread(yl)) + dg(xl, yh_d))

    x = _series_inverse(packed, eye, SCAN_BLOCK, mm)
    blocks = [x[g][:, j * c:(j + 1) * c] for g in range(n // per_row) for j in range(per_row)]
    return jnp.stack([jnp.where(same_block, jnp.concatenate([blk] * (c // SCAN_BLOCK), axis=0), 0.0)
                      for blk in blocks])


def _unit_lower_inverse(a, c):
    row = lax.broadcasted_iota(jnp.int32, (c, c), 0)
    col = lax.broadcasted_iota(jnp.int32, (c, c), 1)
    eye = (row == col).astype(F32)
    same_block = (row // SCAN_BLOCK) == (col // SCAN_BLOCK)
    a_diag = jnp.where(same_block, a, 0.0)
    t_diag = _diag_block_inverse(a_diag, same_block, c)
    mm = functools.partial(_bmm, passes=INV_PASSES)
    b = mm(t_diag, a - a_diag)
    return mm(_series_inverse(b, eye, c // SCAN_BLOCK, mm), t_diag)


def _scan_body(valid, u_ref, sh_ref, mu_ref, w0_ref, w2_ref, a0_ref, a2_ref, g2_ref, kk_ref, ka_ref, hsum_ref,
               s0_ref, rk_ref, lnw_ref, lnb_ref, y_ref, so_ref, s_ref, prev_ref):
    t = pl.program_id(1)
    nb, c, _ = u_ref.shape

    @pl.when(t == 0)
    def _():
        s_ref[...] = s0_ref[...]
        prev_ref[...] = sh_ref[...]

    shifted = []
    for i in range(nb):
        u = u_ref[i]
        ext = jnp.concatenate([prev_ref[i], u], axis=0)
        prev_ref[i] = ext[c:]
        shifted.append(u + (pltpu.roll(ext, 1, 0)[_ROW_PAD:] - u) * mu_ref[...])
    xs = jnp.concatenate(shifted, axis=0)
    low = xs[:, 3 * GROUP_W:]
    w = -_softplus(-(w0_ref[...] + _dot(jnp.tanh(low), w2_ref[...]))) - 0.5
    a = jax.nn.sigmoid(a0_ref[...] + _dot(low, a2_ref[...]))
    k_raw = xs[:, GROUP_W:2 * GROUP_W]
    kk = k_raw * kk_ref[...]
    norm = jnp.sqrt(_dot_split(kk * kk, hsum_ref[...]))
    kap = kk / jnp.maximum(norm, 1e-12)
    steps = [xs[:, 0:GROUP_W],
             -jnp.exp(w),
             k_raw * (1.0 + (a - 1.0) * ka_ref[...]),
             xs[:, 2 * GROUP_W:3 * GROUP_W],
             kap,
             kap * a]
    if valid < c:
        live = lax.broadcasted_iota(jnp.int32, (nb * c, GROUP_W), 0) % c < valid
        steps = [jnp.where(live, x, 0.0) for x in steps]
    r_all, lw, k_all, v_all, kap_all, b_all = (x.reshape(nb, c, GROUP_W) for x in steps)
    gate = _dot(jax.nn.sigmoid(low), g2_ref[...]).reshape(nb, c, GROUP_W)

    row = lax.broadcasted_iota(jnp.int32, (c, c), 0)
    col = lax.broadcasted_iota(jnp.int32, (c, c), 1)
    incl = row >= col
    strict = row > col
    tri = incl.astype(BF16)

    def heads(x):
        return jnp.stack([x[i][:, h * HEAD_DIM:(h + 1) * HEAD_DIM]
                          for i in range(nb) for h in range(N_HEADS)])

    lw_hi, lw_lo = _split2(lw)
    lw_lo2 = (lw - lw_hi.astype(F32) - lw_lo.astype(F32)).astype(BF16)
    cum = jnp.stack([jnp.dot(tri, lw_hi[i], preferred_element_type=F32)
                     + (jnp.dot(tri, lw_lo[i], preferred_element_type=F32)
                        + jnp.dot(tri, lw_lo2[i], preferred_element_type=F32)) for i in range(nb)])
    cmid = cum[:, c // 2 - 1:c // 2, :]
    cend = cum[:, c - 1:c, :]
    e_bwd = jnp.exp(cmid - cum)
    e_end = jnp.exp(cend - cum)
    r_t = heads(r_all * jnp.exp(cum - cmid))
    kap_t = heads(kap_all * jnp.exp(cum - lw - cmid))
    k_t = heads(k_all * e_bwd)
    b_t = heads(b_all * e_bwd)
    r_0 = heads(r_all * jnp.exp(cum))
    kap_0 = heads(kap_all * jnp.exp(cum - lw))
    k_e = heads(k_all * e_end)
    b_e = heads(b_all * e_end)
    g_end = heads(jnp.exp(cend))
    v = heads(v_all)
    s = s_ref[...].reshape(nb * N_HEADS, HEAD_DIM, HEAD_DIM)

    a_kb = jnp.where(strict, _bmm(kap_t, b_t, _NT), 0.0)
    a_kk = jnp.where(strict, _bmm(kap_t, k_t, _NT), 0.0)
    a_rb = jnp.where(incl, _bmm(r_t, b_t, _NT), 0.0)
    a_rk = jnp.where(incl, _bmm(r_t, k_t, _NT), 0.0)
    t_inv = _unit_lower_inverse(a_kb, c)

    u = _bmm(t_inv, _bmm(kap_0, s, _NT, STATE_PASSES) + _bmm(a_kk, v), passes=STATE_PASSES)
    y = _bmm(r_0, s, _NT, STATE_PASSES) + _bmm(a_rk, v) - _bmm(a_rb, u)
    s_new = s * g_end + _bmm(v, k_e, _TN, STATE_PASSES) - _bmm(u, b_e, _TN, STATE_PASSES)
    s_new = s_new.reshape(nb, N_HEADS, HEAD_DIM, HEAD_DIM)
    s_ref[...] = s_new
    so_ref[...] = s_new

    mean = jnp.mean(y, axis=-1, keepdims=True)
    yc = y - mean
    var = jnp.mean(yc * yc, axis=-1, keepdims=True)
    yn = yc * lax.rsqrt(var + GN_EPS)
    bonus = jnp.sum(heads(r_all * k_all * rk_ref[...]), axis=-1, keepdims=True) * v
    for i in range(nb):
        wide = lambda x: jnp.concatenate([x[i * N_HEADS + h] for h in range(N_HEADS)], axis=-1)
        y_ref[i] = ((wide(yn) * lnw_ref[...] + lnb_ref[...] + wide(bonus)) * gate[i]).astype(BF16)


def _rwkv_mixer(u, shift_state, s0, p):
    bsz, tlen, _ = u.shape
    c = SCAN_CHUNK
    nb = SCAN_BATCH
    u = _pad_time(u, c)
    sh = jnp.pad(shift_state[:, None, :], ((0, 0), (_ROW_PAD - 1, 0), (0, 0)))
    st = pl.BlockSpec((nb, N_HEADS, HEAD_DIM, HEAD_DIM), lambda b_, t: (b_, 0, 0, 0))
    vec = _full((1, GROUP_W))
    low = _full((RWKV_LOWRANK_W, GROUP_W))
    y, state = pl.pallas_call(
        functools.partial(_scan_body, min(tlen, c)),
        grid=(bsz // nb, u.shape[1] // c),
        in_specs=[pl.BlockSpec((nb, c, RWKV_IN), lambda b_, t: (b_, t, 0)),
                  pl.BlockSpec((nb, _ROW_PAD, RWKV_IN), lambda b_, t: (b_, 0, 0)),
                  _full((1, RWKV_IN)), vec, low, vec, low, low, vec, vec, _full((GROUP_W, GROUP_W)),
                  st, vec, vec, vec],
        out_specs=[pl.BlockSpec((nb, c, GROUP_W), lambda b_, t: (b_, t, 0)), st],
        out_shape=[jax.ShapeDtypeStruct((bsz, u.shape[1], GROUP_W), BF16),
                   jax.ShapeDtypeStruct((bsz, N_HEADS, HEAD_DIM, HEAD_DIM), F32)],
        scratch_shapes=[pltpu.VMEM((nb, N_HEADS, HEAD_DIM, HEAD_DIM), F32),
                        pltpu.VMEM((nb, _ROW_PAD, RWKV_IN), F32)],
        compiler_params=_cparams(("arbitrary", "arbitrary")),
        name="rwkv_mixer",
    )(u, sh, p["mu"], p["w0"], p["w2p"], p["a0"], p["a2p"], p["g2p"], p["k_k"], p["k_a"], p["hsum"],
      s0, p["r_k"], p["ln_w"], p["ln_b"])
    return y[:, :tlen], state


def _attn_body(past, q_ref, k_ref, v_ref, o_ref, kb_ref, vb_ref):
    qi = pl.program_id(1)
    tq = q_ref.shape[1]
    tk = ATTN_TILE
    q0 = past + qi * tq
    diag = q0 // tk
    scale = HEAD_DIM ** -0.5 * LOG2_E

    @pl.when(qi == 0)
    def _():
        kb_ref[...] = k_ref[0].astype(BF16)
        vb_ref[...] = v_ref[0].astype(BF16)

    heads = [slice(h * HEAD_DIM, (h + 1) * HEAD_DIM) for h in range(N_HEADS)]
    qs = [(q_ref[0, :, sl] * scale).astype(BF16) for sl in heads]
    later = (lax.broadcasted_iota(jnp.int32, (tk, tk), 0)
             > lax.broadcasted_iota(jnp.int32, (tk, tk), 1)).astype(BF16)
    every = range(N_HEADS)

    def tiles(js, carry, masked):
        accs, runs = carry
        units = [(t, h) for t in range(len(js)) for h in every]
        kt = [kb_ref[pl.ds(pl.multiple_of(j * tk, tk), tk), :] for j in js]
        vt = [vb_ref[pl.ds(pl.multiple_of(j * tk, tk), tk), :] for j in js]
        if masked:
            vis = [(j * tk + lax.broadcasted_iota(jnp.int32, (tq, tk), 1)
                    < q0 + lax.broadcasted_iota(jnp.int32, (tq, tk), 0)) for j in js]
        z = [_dot_nt(qs[h], kt[t][:, heads[h]]) for t, h in units]
        log_take = [jnp.minimum(x, 0.0) - jnp.log2(1.0 + jnp.exp2(-jnp.abs(x))) for x in z]
        log_keep = [a - b for a, b in zip(log_take, z)]
        if masked:
            log_keep = [jnp.where(vis[t], x, 0.0) for (t, _), x in zip(units, log_keep)]
        after = [_dot(x, later) for x in log_keep]
        p = [jnp.exp2(a + b) for a, b in zip(log_take, after)]
        if masked:
            p = [jnp.where(vis[t], x, 0.0) for (t, _), x in zip(units, p)]
        part = [jnp.dot(x.astype(BF16), vt[t][:, heads[h]], preferred_element_type=F32)
                for (t, h), x in zip(units, p)]
        accs, runs = list(accs), list(runs)
        for n, (_, h) in enumerate(units):
            accs[h] = accs[h] + jnp.exp2(runs[h]) * part[n]
            runs[h] = runs[h] + (after[n][:, 0:1] + log_keep[n][:, 0:1])
        return tuple(accs), tuple(runs)

    carry = (tuple(jnp.zeros((tq, HEAD_DIM), F32) for _ in every),
             tuple(jnp.zeros((tq, 1), F32) for _ in every))
    carry = tiles([diag], carry, True)
    single = diag % ATTN_UNROLL
    carry = lax.fori_loop(0, single, lambda i, c: tiles([diag - 1 - i], c, False), carry)
    first = diag - single - 1
    accs, _ = lax.fori_loop(
        0, diag // ATTN_UNROLL,
        lambda i, c: tiles([first - ATTN_UNROLL * i - t for t in range(ATTN_UNROLL)], c, False), carry)
    o_ref[0] = jnp.concatenate(accs, axis=-1).astype(BF16)


def _sb_attn(q, k_all, v_all, past):
    bsz, tlen, _ = q.shape
    tq = min(ATTN_TILE, tlen)
    slen = k_all.shape[1]
    kv = pl.BlockSpec((1, slen, GROUP_W), lambda b, i: (b, 0, 0))
    return pl.pallas_call(
        functools.partial(_attn_body, past),
        grid=(bsz, tlen // tq),
        in_specs=[pl.BlockSpec((1, tq, GROUP_W), lambda b, i: (b, i, 0)), kv, kv],
        out_specs=pl.BlockSpec((1, tq, GROUP_W), lambda b, i: (b, i, 0)),
        out_shape=jax.ShapeDtypeStruct((bsz, tlen, GROUP_W), BF16),
        scratch_shapes=[pltpu.VMEM((slen, GROUP_W), BF16), pltpu.VMEM((slen, GROUP_W), BF16)],
        compiler_params=_cparams(("arbitrary", "arbitrary")),
        name="sb_attn",
    )(q, k_all, v_all)


def _pad_time(x, mult):
    pad = (-x.shape[1]) % mult
    return x if pad == 0 else jnp.pad(x, ((0, 0), (0, pad), (0, 0)))


def _prep_params(prm):
    out = []
    depth = prm["w_in"].shape[0]
    head_of = jnp.arange(GROUP_W) // HEAD_DIM
    hsum = (head_of[:, None] == head_of[None, :]).astype(BF16)
    row2 = lambda v: v.reshape(1, -1)
    for l in range(depth):
        w_bd = jax.scipy.linalg.block_diag(*[prm["pool_w"][l, i] for i in range(len(POOL_WINDOWS))])
        lowpad = lambda w, lo: jnp.zeros((RWKV_LOWRANK_W, GROUP_W), F32).at[lo:lo + w.shape[0]].set(w)
        out.append(dict(
            g=[row2(prm["norm_g"][l, i]) for i in range(6)],
            wg=[prm["ffn_w_gate"][l, i].astype(BF16) for i in range(2)],
            wu=[prm["ffn_w_up"][l, i].astype(BF16) for i in range(2)],
            wd=[prm["ffn_w_down"][l, i].astype(BF16) for i in range(2)],
            w_in=prm["w_in"][l].astype(BF16),
            w_out=prm["w_out"][l].astype(BF16),
            w_bd=w_bd.astype(BF16),
            pool_scale=row2(prm["pool_scale"][l]),
            conv_w=prm["conv_w"][l],
            mu=row2(prm["rwkv_mu"][l]),
            w0=row2(prm["rwkv_w0"][l]),
            w2p=lowpad(prm["rwkv_w2"][l], 0).astype(BF16),
            a0=row2(prm["rwkv_a0"][l]),
            a2p=lowpad(prm["rwkv_a2"][l], 32).astype(BF16),
            g2p=lowpad(prm["rwkv_g2"][l], 64).astype(BF16),
            k_k=row2(prm["rwkv_k_k"][l]),
            k_a=row2(prm["rwkv_k_a"][l]),
            r_k=row2(prm["rwkv_r_k"][l]),
            ln_w=row2(prm["rwkv_ln_w"][l]),
            ln_b=row2(prm["rwkv_ln_b"][l]),
            hsum=hsum,
        ))
    return out


def _trunk(x, k_past, v_past, wkv0, shift0, conv0, pool0, layers, past):
    bsz, tlen, _ = x.shape
    n = bsz * tlen
    x = x.reshape(n, D_MODEL)
    ks, vs, wkvs, shifts, convs, pools = [], [], [], [], [], []
    for l, p in enumerate(layers):
        x = _ffn(x, p["g"][0], p["g"][1], p["wg"][0], p["wu"][0], p["wd"][0])
        u_pool, u_rwkv, u_conv, q, k, v = _inproj(x, p["g"][2], p["w_in"])
        seq = lambda a: a.reshape(bsz, tlen, -1)
        u_pool, u_rwkv, u_conv, q, k, v = map(seq, (u_pool, u_rwkv, u_conv, q, k, v))

        y_pool, y_conv, conv_new = _local(u_pool, u_conv, pool0[l], conv0[l], p["w_bd"],
                                          p["pool_scale"], p["conv_w"], past)

        y_rwkv, wkv_new = _rwkv_mixer(u_rwkv, shift0[l], wkv0[l], p)

        if past:
            with_past = lambda old, new: _pad_time(
                jnp.concatenate([old[l].reshape(bsz, past, GROUP_W), new], axis=1), ATTN_TILE)
            y_sb = _sb_attn(q, with_past(k_past, k), with_past(v_past, v), past)
        else:
            y_sb = _sb_attn(q, k, v, past)

        flat = lambda a: a.reshape(n, GROUP_W)
        x = _ffn(x, p["g"][4], p["g"][5], p["wg"][1], p["wu"][1], p["wd"][1],
                 mix=(flat(y_pool), flat(y_rwkv), flat(y_conv), flat(y_sb), p["w_out"], p["g"][3]))

        ks.append(k.reshape(bsz, tlen, N_HEADS, HEAD_DIM))
        vs.append(v.reshape(bsz, tlen, N_HEADS, HEAD_DIM))
        wkvs.append(wkv_new)
        shifts.append(u_rwkv[:, -1])
        convs.append(conv_new)
        pools.append(u_pool[:, -POOL_STATE:])
    return (x.reshape(bsz, tlen, D_MODEL), jnp.stack(ks), jnp.stack(vs), jnp.stack(wkvs),
            jnp.stack(shifts), jnp.stack(convs), jnp.stack(pools))


def kernel(x_prompt, x_sample, cache_sb_k, cache_sb_v, state_wkv, state_shift, state_conv, state_pool,
           norm_g, ffn_w_gate, ffn_w_up, ffn_w_down, w_in, w_out, pool_w, pool_scale,
           rwkv_mu, rwkv_w0, rwkv_w2, rwkv_a0, rwkv_a2, rwkv_g2, rwkv_k_k, rwkv_k_a, rwkv_r_k,
           rwkv_ln_w, rwkv_ln_b, conv_w):
    layers = _prep_params(dict(
        norm_g=norm_g, ffn_w_gate=ffn_w_gate, ffn_w_up=ffn_w_up, ffn_w_down=ffn_w_down, w_in=w_in,
        w_out=w_out, pool_w=pool_w, pool_scale=pool_scale, rwkv_mu=rwkv_mu, rwkv_w0=rwkv_w0,
        rwkv_w2=rwkv_w2, rwkv_a0=rwkv_a0, rwkv_a2=rwkv_a2, rwkv_g2=rwkv_g2, rwkv_k_k=rwkv_k_k,
        rwkv_k_a=rwkv_k_a, rwkv_r_k=rwkv_r_k, rwkv_ln_w=rwkv_ln_w, rwkv_ln_b=rwkv_ln_b, conv_w=conv_w))
    depth = w_in.shape[0]
    bsz = x_prompt.shape[0]
    z_wkv = jnp.zeros((depth, bsz, N_HEADS, HEAD_DIM, HEAD_DIM), F32)
    z_shift = jnp.zeros((depth, bsz, RWKV_IN), F32)
    z_conv = jnp.zeros((depth, bsz, CONV_W - 1, GROUP_W), F32)
    z_pool = jnp.zeros((depth, bsz, POOL_STATE, GROUP_W), F32)
    prompt = _trunk(x_prompt, None, None, z_wkv, z_shift, z_conv, z_pool, layers, 0)
    sample = _trunk(x_sample, cache_sb_k, cache_sb_v, state_wkv, state_shift, state_conv, state_pool,
                    layers, cache_sb_k.shape[2])
    return (prompt[0], sample[0]) + prompt[1:] + sample[1:]
```

```python
import functools

import jax
import jax.numpy as jnp
from jax import lax
from jax.experimental import pallas as pl
from jax.experimental.pallas import tpu as pltpu

F32 = jnp.float32
BF16 = jnp.bfloat16

D_MODEL = 1024
GROUP_W = 256
HEAD_DIM = 64
N_HEADS = 4
POOL_WINDOWS = (2, 4, 8, 16)
POOL_STATE = 15
CONV_W = 3
RWKV_IN = 896
RWKV_LOWRANK_W = 128
D_FF = 2816
RMS_EPS = 1e-6
GN_EPS = 64e-5
LOG2_E = 1.4426950408889634

FF_CHUNK = 256
TOKEN_TILE = 512
SCAN_CHUNK = 64
SCAN_BLOCK = 16
PACK_W = 256
SCAN_BATCH = 8
INV_PASSES = 3
STATE_PASSES = 3
ATTN_TILE = 256
ATTN_UNROLL = 4
VMEM_LIMIT = 56 * 1024 * 1024


def _cparams(sem):
    return pltpu.CompilerParams(dimension_semantics=sem, vmem_limit_bytes=VMEM_LIMIT)


def _rms(x, g):
    return x * lax.rsqrt(jnp.mean(x * x, axis=-1, keepdims=True) + RMS_EPS) * g


def _dot(a, b):
    return jnp.dot(a.astype(BF16), b.astype(BF16), preferred_element_type=F32)


def _dot_nt(a, b):
    return lax.dot_general(a, b, (((1,), (1,)), ((), ())), preferred_element_type=F32)


def _split2(a):
    hi = a.astype(BF16)
    return hi, (a - hi.astype(F32)).astype(BF16)


def _dot_split(a, ones_b):
    hi, lo = _split2(a)
    return (jnp.dot(hi, ones_b, preferred_element_type=F32)
            + jnp.dot(lo, ones_b, preferred_element_type=F32))


def _full(shape):
    return pl.BlockSpec(shape, lambda *_: (0,) * len(shape))


def _resident(shape):
    return pl.BlockSpec(shape, lambda *_: (0,) * len(shape), pipeline_mode=pl.Buffered(1))


def _ffn_body(mixed, x_ref, *refs):
    if mixed:
        mix_refs, wo_ref, gmix_ref = refs[:4], refs[4], refs[5]
        refs = refs[6:]
        y = sum(jnp.dot(m_ref[...], wo_ref[i * GROUP_W:(i + 1) * GROUP_W, :], preferred_element_type=F32)
                for i, m_ref in enumerate(mix_refs))
        x = x_ref[...] + _rms(y, gmix_ref[...])
    else:
        x = x_ref[...]
    gpre_ref, gpost_ref, wg_ref, wu_ref, wd_ref, o_ref, a_ref = refs
    hb = _rms(x, gpre_ref[...]).astype(BF16)
    for j in range(D_FF // FF_CHUNK):
        sl = slice(j * FF_CHUNK, (j + 1) * FF_CHUNK)
        g = jnp.dot(hb, wg_ref[:, sl], preferred_element_type=F32)
        u = jnp.dot(hb, wu_ref[:, sl], preferred_element_type=F32)
        a_ref[:, sl] = (g * jax.nn.sigmoid(g) * u).astype(BF16)
    y = jnp.dot(a_ref[...], wd_ref[...], preferred_element_type=F32)
    o_ref[...] = x + 0.5 * _rms(y, gpost_ref[...])


def _ffn(x, g_pre, g_post, wg, wu, wd, mix=None):
    n = x.shape[0]
    tm = min(TOKEN_TILE, n)
    row = pl.BlockSpec((tm, D_MODEL), lambda i: (i, 0))
    grp = pl.BlockSpec((tm, GROUP_W), lambda i: (i, 0))
    mix_specs = [] if mix is None else [grp] * 4 + [_resident((D_MODEL, D_MODEL)), _full((1, D_MODEL))]
    return pl.pallas_call(
        functools.partial(_ffn_body, mix is not None),
        grid=(n // tm,),
        in_specs=[row] + mix_specs + [_full((1, D_MODEL)), _full((1, D_MODEL)), _resident((D_MODEL, D_FF)),
                                      _resident((D_MODEL, D_FF)), _resident((D_FF, D_MODEL))],
        out_specs=row,
        out_shape=jax.ShapeDtypeStruct((n, D_MODEL), F32),
        scratch_shapes=[pltpu.VMEM((tm, D_FF), BF16)],
        compiler_params=_cparams(("arbitrary",)),
        name="ffn",
    )(x, *([] if mix is None else mix), g_pre, g_post, wg, wu, wd)


_PROJ_EDGES = (0, 256, 1152, 1920, 2176, 2432, 2688)
_POOL_PAD = 16
_ROW_PAD = 8


def _inproj_body(past, tiles_per_seq, x_ref, g_ref, w_ref, pst_ref, cst_ref, wbd_ref, scale_ref, cw_ref,
                 yp_ref, ur_ref, yc_ref, q_ref, k_ref, v_ref, pso_ref, cso_ref, pprev_ref, cprev_ref):
    t = pl.program_id(0) % tiles_per_seq
    tm = x_ref.shape[0]

    @pl.when(t == 0)
    def _():
        pprev_ref[...] = pst_ref[0]
        cprev_ref[...] = cst_ref[0]

    hb = _rms(x_ref[...], g_ref[...]).astype(BF16)
    proj = lambda i: jnp.dot(hb, w_ref[:, _PROJ_EDGES[i]:_PROJ_EDGES[i + 1]], preferred_element_type=F32)
    ur_ref[...] = proj(1)
    q_ref[...] = proj(3)
    k_ref[...] = proj(4)
    v_ref[...] = proj(5)

    u = proj(0)
    ext = jnp.concatenate([pprev_ref[...], u], axis=0)
    s2 = ext + pltpu.roll(ext, 1, 0)
    s4 = s2 + pltpu.roll(s2, 2, 0)
    s8 = s4 + pltpu.roll(s4, 4, 0)
    s16 = s8 + pltpu.roll(s8, 8, 0)
    grp = lax.broadcasted_iota(jnp.int32, (tm, GROUP_W), 1) // (GROUP_W // len(POOL_WINDOWS))
    pos1 = past + t * tm + lax.broadcasted_iota(jnp.int32, (tm, GROUP_W), 0) + 1
    win = jnp.where(grp == 0, s2[_POOL_PAD:], jnp.where(grp == 1, s4[_POOL_PAD:],
                    jnp.where(grp == 2, s8[_POOL_PAD:], s16[_POOL_PAD:])))
    width = jnp.where(grp == 0, 2, jnp.where(grp == 1, 4, jnp.where(grp == 2, 8, 16)))
    cnt = jnp.minimum(width, pos1).astype(F32)
    d = win / cnt - u
    yp_ref[...] = (_dot(d, wbd_ref[...]) * scale_ref[...]).astype(BF16)
    pprev_ref[...] = ext[tm:]
    pso_ref[0] = ext[tm:]

    uc = proj(2)
    b = uc[:, 0:GROUP_W]
    z = uc[:, GROUP_W:2 * GROUP_W] * uc[:, 2 * GROUP_W:3 * GROUP_W]
    extz = jnp.concatenate([cprev_ref[...], z], axis=0)
    y = (pltpu.roll(extz, 2, 0)[_ROW_PAD:] * cw_ref[0:1, :]
         + pltpu.roll(extz, 1, 0)[_ROW_PAD:] * cw_ref[1:2, :]
         + z * cw_ref[2:3, :])
    yc_ref[...] = (b * y).astype(BF16)
    cprev_ref[...] = extz[tm:]
    cso_ref[0] = extz[tm:][_ROW_PAD - (CONV_W - 1):]


def _inproj(x, tlen, g, w_in, pool_state, conv_state, w_bd, scale, conv_w, past):
    n = x.shape[0]
    bsz = n // tlen
    tm = min(TOKEN_TILE, tlen)
    tiles_per_seq = tlen // tm
    pst = jnp.pad(pool_state, ((0, 0), (_POOL_PAD - POOL_STATE, 0), (0, 0)))
    cst = jnp.pad(conv_state, ((0, 0), (_ROW_PAD - (CONV_W - 1), 0), (0, 0)))
    row = lambda w: pl.BlockSpec((tm, w), lambda i: (i, 0))
    per_seq = lambda r: pl.BlockSpec((1, r, GROUP_W), lambda i: (i // tiles_per_seq, 0, 0))
    flat = lambda w, dt: jax.ShapeDtypeStruct((n, w), dt)
    return pl.pallas_call(
        functools.partial(_inproj_body, past, tiles_per_seq),
        grid=(n // tm,),
        in_specs=[row(D_MODEL), _full((1, D_MODEL)), _resident((D_MODEL, _PROJ_EDGES[-1])),
                  per_seq(_POOL_PAD), per_seq(_ROW_PAD),
                  _full((GROUP_W, GROUP_W)), _full((1, GROUP_W)), _full((CONV_W, GROUP_W))],
        out_specs=[row(GROUP_W), row(RWKV_IN), row(GROUP_W), row(GROUP_W), row(GROUP_W), row(GROUP_W),
                   per_seq(_POOL_PAD), per_seq(CONV_W - 1)],
        out_shape=[flat(GROUP_W, BF16), flat(RWKV_IN, F32), flat(GROUP_W, BF16), flat(GROUP_W, F32),
                   flat(GROUP_W, F32), flat(GROUP_W, F32),
                   jax.ShapeDtypeStruct((bsz, _POOL_PAD, GROUP_W), F32),
                   jax.ShapeDtypeStruct((bsz, CONV_W - 1, GROUP_W), F32)],
        scratch_shapes=[pltpu.VMEM((_POOL_PAD, GROUP_W), F32), pltpu.VMEM((_ROW_PAD, GROUP_W), F32)],
        compiler_params=_cparams(("arbitrary",)),
        name="inproj",
    )(x, g, w_in, pst, cst, w_bd, scale, conv_w)


def _softplus(x):
    return jnp.maximum(x, 0.0) + jnp.log1p(jnp.exp(-jnp.abs(x)))


_NN = ((2,), (1,))
_NT = ((2,), (2,))
_TN = ((1,), (1,))


def _bmm(a, b, dims=_NN, passes=1):
    dg = lambda x, y: lax.dot_general(x, y, (dims, ((0,), (0,))), preferred_element_type=F32)
    if passes == 1:
        return dg(a.astype(BF16), b.astype(BF16))
    ah, al = _split2(a)
    bh, bl = _split2(b)
    if dims == _TN:
        return dg(ah, bh) + (dg(ah, bl) + dg(al, bh))
    m = a.shape[1]
    both = dg(jnp.concatenate([ah, al], axis=1), bh)
    return both[:, :m] + (both[:, m:] + dg(ah, bl))


def _series_inverse(m, eye, order, mm):
    x = eye - m
    p = m
    n = 2
    while n < order:
        p = mm(p, p)
        x = x + mm(x, p)
        n *= 2
    return x


def _diag_block_inverse(a_diag, same_block, c):
    n = a_diag.shape[0]
    per_row = PACK_W // c
    folded = sum(a_diag[:, i * SCAN_BLOCK:(i + 1) * SCAN_BLOCK, :] for i in range(c // SCAN_BLOCK))
    packed = jnp.stack([jnp.concatenate([folded[g * per_row + j] for j in range(per_row)], axis=-1)
                        for g in range(n // per_row)])
    lane = lax.broadcasted_iota(jnp.int32, (SCAN_BLOCK, PACK_W), 1)
    eye = (lax.broadcasted_iota(jnp.int32, (SCAN_BLOCK, PACK_W), 0) == lane % SCAN_BLOCK).astype(F32)
    on_diag = ((lax.broadcasted_iota(jnp.int32, (PACK_W, PACK_W), 0) // SCAN_BLOCK)
               == (lax.broadcasted_iota(jnp.int32, (PACK_W, PACK_W), 1) // SCAN_BLOCK)).astype(BF16)

    def spread(y):
        return jnp.concatenate([y] * (PACK_W // SCAN_BLOCK), axis=1) * on_diag

    def mm(x, y):
        dg = lambda p, q: lax.dot_general(p, q, (_NN, ((0,), (0,))), preferred_element_type=F32)
        if INV_PASSES == 1:
            return dg(x.astype(BF16), spread(y.astype(BF16)))
        xh, xl = _split2(x)
        yh, yl = _split2(y)
        yh_d = spread(yh)
        return dg(xh, yh_d) + (dg(xh, spread(yl)) + dg(xl, yh_d))

    x = _series_inverse(packed, eye, SCAN_BLOCK, mm)
    blocks = [x[g][:, j * c:(j + 1) * c] for g in range(n // per_row) for j in range(per_row)]
    return jnp.stack([jnp.where(same_block, jnp.concatenate([blk] * (c // SCAN_BLOCK), axis=0), 0.0)
                      for blk in blocks])


def _unit_lower_inverse(a, c):
    row = lax.broadcasted_iota(jnp.int32, (c, c), 0)
    col = lax.broadcasted_iota(jnp.int32, (c, c), 1)
    eye = (row == col).astype(F32)
    same_block = (row // SCAN_BLOCK) == (col // SCAN_BLOCK)
    a_diag = jnp.where(same_block, a, 0.0)
    t_diag = _diag_block_inverse(a_diag, same_block, c)
    mm = functools.partial(_bmm, passes=INV_PASSES)
    b = mm(t_diag, a - a_diag)
    return mm(_series_inverse(b, eye, c // SCAN_BLOCK, mm), t_diag)


def _scan_body(valid, u_ref, sh_ref, mu_ref, w0_ref, w2_ref, a0_ref, a2_ref, g2_ref, kk_ref, ka_ref, hsum_ref,
               s0_ref, rk_ref, lnw_ref, lnb_ref, y_ref, so_ref, s_ref, prev_ref):
    t = pl.program_id(1)
    nb, c, _ = u_ref.shape

    @pl.when(t == 0)
    def _():
        s_ref[...] = s0_ref[...]
        prev_ref[...] = sh_ref[...]

    shifted = []
    for i in range(nb):
        u = u_ref[i]
        ext = jnp.concatenate([prev_ref[i], u], axis=0)
        prev_ref[i] = ext[c:]
        shifted.append(u + (pltpu.roll(ext, 1, 0)[_ROW_PAD:] - u) * mu_ref[...])
    xs = jnp.concatenate(shifted, axis=0)
    low = xs[:, 3 * GROUP_W:]
    w = -_softplus(-(w0_ref[...] + _dot(jnp.tanh(low), w2_ref[...]))) - 0.5
    a = jax.nn.sigmoid(a0_ref[...] + _dot(low, a2_ref[...]))
    k_raw = xs[:, GROUP_W:2 * GROUP_W]
    kk = k_raw * kk_ref[...]
    norm = jnp.sqrt(_dot_split(kk * kk, hsum_ref[...]))
    kap = kk / jnp.maximum(norm, 1e-12)
    steps = [xs[:, 0:GROUP_W],
             -jnp.exp(w),
             k_raw * (1.0 + (a - 1.0) * ka_ref[...]),
             xs[:, 2 * GROUP_W:3 * GROUP_W],
             kap,
             kap * a]
    if valid < c:
        live = lax.broadcasted_iota(jnp.int32, (nb * c, GROUP_W), 0) % c < valid
        steps = [jnp.where(live, x, 0.0) for x in steps]
    r_all, lw, k_all, v_all, kap_all, b_all = (x.reshape(nb, c, GROUP_W) for x in steps)
    gate = _dot(jax.nn.sigmoid(low), g2_ref[...]).reshape(nb, c, GROUP_W)

    row = lax.broadcasted_iota(jnp.int32, (c, c), 0)
    col = lax.broadcasted_iota(jnp.int32, (c, c), 1)
    incl = row >= col
    strict = row > col
    tri = incl.astype(BF16)

    def heads(x):
        return jnp.stack([x[i][:, h * HEAD_DIM:(h + 1) * HEAD_DIM]
                          for i in range(nb) for h in range(N_HEADS)])

    lw_hi, lw_lo = _split2(lw)
    lw_lo2 = (lw - lw_hi.astype(F32) - lw_lo.astype(F32)).astype(BF16)
    cum = jnp.stack([jnp.dot(tri, lw_hi[i], preferred_element_type=F32)
                     + (jnp.dot(tri, lw_lo[i], preferred_element_type=F32)
                        + jnp.dot(tri, lw_lo2[i], preferred_element_type=F32)) for i in range(nb)])
    cmid = cum[:, c // 2 - 1:c // 2, :]
    cend = cum[:, c - 1:c, :]
    e_bwd = jnp.exp(cmid - cum)
    e_end = jnp.exp(cend - cum)
    r_t = heads(r_all * jnp.exp(cum - cmid))
    kap_t = heads(kap_all * jnp.exp(cum - lw - cmid))
    k_t = heads(k_all * e_bwd)
    b_t = heads(b_all * e_bwd)
    r_0 = heads(r_all * jnp.exp(cum))
    kap_0 = heads(kap_all * jnp.exp(cum - lw))
    k_e = heads(k_all * e_end)
    b_e = heads(b_all * e_end)
    g_end = heads(jnp.exp(cend))
    v = heads(v_all)
    s = s_ref[...].reshape(nb * N_HEADS, HEAD_DIM, HEAD_DIM)

    a_kb = jnp.where(strict, _bmm(kap_t, b_t, _NT), 0.0)
    a_kk = jnp.where(strict, _bmm(kap_t, k_t, _NT), 0.0)
    a_rb = jnp.where(incl, _bmm(r_t, b_t, _NT), 0.0)
    a_rk = jnp.where(incl, _bmm(r_t, k_t, _NT), 0.0)
    t_inv = _unit_lower_inverse(a_kb, c)

    u = _bmm(t_inv, _bmm(kap_0, s, _NT, STATE_PASSES) + _bmm(a_kk, v), passes=STATE_PASSES)
    y = _bmm(r_0, s, _NT, STATE_PASSES) + _bmm(a_rk, v) - _bmm(a_rb, u)
    s_new = s * g_end + _bmm(v, k_e, _TN, STATE_PASSES) - _bmm(u, b_e, _TN, STATE_PASSES)
    s_new = s_new.reshape(nb, N_HEADS, HEAD_DIM, HEAD_DIM)
    s_ref[...] = s_new
    so_ref[...] = s_new

    mean = jnp.mean(y, axis=-1, keepdims=True)
    yc = y - mean
    var = jnp.mean(yc * yc, axis=-1, keepdims=True)
    yn = yc * lax.rsqrt(var + GN_EPS)
    bonus = jnp.sum(heads(r_all * k_all * rk_ref[...]), axis=-1, keepdims=True) * v
    for i in range(nb):
        wide = lambda x: jnp.concatenate([x[i * N_HEADS + h] for h in range(N_HEADS)], axis=-1)
        y_ref[i] = ((wide(yn) * lnw_ref[...] + lnb_ref[...] + wide(bonus)) * gate[i]).astype(BF16)


def _rwkv_mixer(u, shift_state, s0, p):
    bsz, tlen, _ = u.shape
    c = SCAN_CHUNK
    nb = SCAN_BATCH
    u = _pad_time(u, c)
    sh = jnp.pad(shift_state[:, None, :], ((0, 0), (_ROW_PAD - 1, 0), (0, 0)))
    st = pl.BlockSpec((nb, N_HEADS, HEAD_DIM, HEAD_DIM), lambda b_, t: (b_, 0, 0, 0))
    vec = _full((1, GROUP_W))
    low = _full((RWKV_LOWRANK_W, GROUP_W))
    y, state = pl.pallas_call(
        functools.partial(_scan_body, min(tlen, c)),
        grid=(bsz // nb, u.shape[1] // c),
        in_specs=[pl.BlockSpec((nb, c, RWKV_IN), lambda b_, t: (b_, t, 0)),
                  pl.BlockSpec((nb, _ROW_PAD, RWKV_IN), lambda b_, t: (b_, 0, 0)),
                  _full((1, RWKV_IN)), vec, low, vec, low, low, vec, vec, _full((GROUP_W, GROUP_W)),
                  st, vec, vec, vec],
        out_specs=[pl.BlockSpec((nb, c, GROUP_W), lambda b_, t: (b_, t, 0)), st],
        out_shape=[jax.ShapeDtypeStruct((bsz, u.shape[1], GROUP_W), BF16),
                   jax.ShapeDtypeStruct((bsz, N_HEADS, HEAD_DIM, HEAD_DIM), F32)],
        scratch_shapes=[pltpu.VMEM((nb, N_HEADS, HEAD_DIM, HEAD_DIM), F32),
                        pltpu.VMEM((nb, _ROW_PAD, RWKV_IN), F32)],
        compiler_params=_cparams(("arbitrary", "arbitrary")),
        name="rwkv_mixer",
    )(u, sh, p["mu"], p["w0"], p["w2p"], p["a0"], p["a2p"], p["g2p"], p["k_k"], p["k_a"], p["hsum"],
      s0, p["r_k"], p["ln_w"], p["ln_b"])
    return y[:, :tlen], state


def _attn_body(past, q_ref, k_ref, v_ref, o_ref, kb_ref, vb_ref):
    qi = pl.program_id(1)
    tq = q_ref.shape[1]
    tk = ATTN_TILE
    q0 = past + qi * tq
    diag = q0 // tk
    scale = HEAD_DIM ** -0.5 * LOG2_E

    @pl.when(qi == 0)
    def _():
        kb_ref[...] = k_ref[0].astype(BF16)
        vb_ref[...] = v_ref[0].astype(BF16)

    heads = [slice(h * HEAD_DIM, (h + 1) * HEAD_DIM) for h in range(N_HEADS)]
    qs = [(q_ref[0, :, sl] * scale).astype(BF16) for sl in heads]
    later = (lax.broadcasted_iota(jnp.int32, (tk, tk), 0)
             > lax.broadcasted_iota(jnp.int32, (tk, tk), 1)).astype(BF16)
    every = range(N_HEADS)

    def tiles(js, carry, masked):
        accs, runs = carry
        units = [(t, h) for t in range(len(js)) for h in every]
        kt = [kb_ref[pl.ds(pl.multiple_of(j * tk, tk), tk), :] for j in js]
        vt = [vb_ref[pl.ds(pl.multiple_of(j * tk, tk), tk), :] for j in js]
        if masked:
            vis = [(j * tk + lax.broadcasted_iota(jnp.int32, (tq, tk), 1)
                    < q0 + lax.broadcasted_iota(jnp.int32, (tq, tk), 0)) for j in js]
        z = [_dot_nt(qs[h], kt[t][:, heads[h]]) for t, h in units]
        log_take = [jnp.minimum(x, 0.0) - jnp.log2(1.0 + jnp.exp2(-jnp.abs(x))) for x in z]
        log_keep = [a - b for a, b in zip(log_take, z)]
        if masked:
            log_keep = [jnp.where(vis[t], x, 0.0) for (t, _), x in zip(units, log_keep)]
        after = [_dot(x, later) for x in log_keep]
        p = [jnp.exp2(a + b) for a, b in zip(log_take, after)]
        if masked:
            p = [jnp.where(vis[t], x, 0.0) for (t, _), x in zip(units, p)]
        part = [jnp.dot(x.astype(BF16), vt[t][:, heads[h]], preferred_element_type=F32)
                for (t, h), x in zip(units, p)]
        accs, runs = list(accs), list(runs)
        for n, (_, h) in enumerate(units):
            accs[h] = accs[h] + jnp.exp2(runs[h]) * part[n]
            runs[h] = runs[h] + (after[n][:, 0:1] + log_keep[n][:, 0:1])
        return tuple(accs), tuple(runs)

    carry = (tuple(jnp.zeros((tq, HEAD_DIM), F32) for _ in every),
             tuple(jnp.zeros((tq, 1), F32) for _ in every))
    carry = tiles([diag], carry, True)
    single = diag % ATTN_UNROLL
    carry = lax.fori_loop(0, single, lambda i, c: tiles([diag - 1 - i], c, False), carry)
    first = diag - single - 1
    accs, _ = lax.fori_loop(
        0, diag // ATTN_UNROLL,
        lambda i, c: tiles([first - ATTN_UNROLL * i - t for t in range(ATTN_UNROLL)], c, False), carry)
    o_ref[0] = jnp.concatenate(accs, axis=-1).astype(BF16)


def _sb_attn(q, k_all, v_all, past):
    bsz, tlen, _ = q.shape
    tq = min(ATTN_TILE, tlen)
    slen = k_all.shape[1]
    kv = pl.BlockSpec((1, slen, GROUP_W), lambda b, i: (b, 0, 0))
    return pl.pallas_call(
        functools.partial(_attn_body, past),
        grid=(bsz, tlen // tq),
        in_specs=[pl.BlockSpec((1, tq, GROUP_W), lambda b, i: (b, i, 0)), kv, kv],
        out_specs=pl.BlockSpec((1, tq, GROUP_W), lambda b, i: (b, i, 0)),
        out_shape=jax.ShapeDtypeStruct((bsz, tlen, GROUP_W), BF16),
        scratch_shapes=[pltpu.VMEM((slen, GROUP_W), BF16), pltpu.VMEM((slen, GROUP_W), BF16)],
        compiler_params=_cparams(("arbitrary", "arbitrary")),
        name="sb_attn",
    )(q, k_all, v_all)


def _pad_time(x, mult):
    pad = (-x.shape[1]) % mult
    return x if pad == 0 else jnp.pad(x, ((0, 0), (0, pad), (0, 0)))


def _prep_params(prm):
    out = []
    depth = prm["w_in"].shape[0]
    head_of = jnp.arange(GROUP_W) // HEAD_DIM
    hsum = (head_of[:, None] == head_of[None, :]).astype(BF16)
    row2 = lambda v: v.reshape(1, -1)
    for l in range(depth):
        w_bd = jax.scipy.linalg.block_diag(*[prm["pool_w"][l, i] for i in range(len(POOL_WINDOWS))])
        lowpad = lambda w, lo: jnp.zeros((RWKV_LOWRANK_W, GROUP_W), F32).at[lo:lo + w.shape[0]].set(w)
        out.append(dict(
            g=[row2(prm["norm_g"][l, i]) for i in range(6)],
            wg=[prm["ffn_w_gate"][l, i].astype(BF16) for i in range(2)],
            wu=[prm["ffn_w_up"][l, i].astype(BF16) for i in range(2)],
            wd=[prm["ffn_w_down"][l, i].astype(BF16) for i in range(2)],
            w_in=prm["w_in"][l].astype(BF16),
            w_out=prm["w_out"][l].astype(BF16),
            w_bd=w_bd.astype(BF16),
            pool_scale=row2(prm["pool_scale"][l]),
            conv_w=prm["conv_w"][l],
            mu=row2(prm["rwkv_mu"][l]),
            w0=row2(prm["rwkv_w0"][l]),
            w2p=lowpad(prm["rwkv_w2"][l], 0).astype(BF16),
            a0=row2(prm["rwkv_a0"][l]),
            a2p=lowpad(prm["rwkv_a2"][l], 32).astype(BF16),
            g2p=lowpad(prm["rwkv_g2"][l], 64).astype(BF16),
            k_k=row2(prm["rwkv_k_k"][l]),
            k_a=row2(prm["rwkv_k_a"][l]),
            r_k=row2(prm["rwkv_r_k"][l]),
            ln_w=row2(prm["rwkv_ln_w"][l]),
            ln_b=row2(prm["rwkv_ln_b"][l]),
            hsum=hsum,
        ))
    return out


def _trunk(x, k_past, v_past, wkv0, shift0, conv0, pool0, layers, past):
    bsz, tlen, _ = x.shape
    n = bsz * tlen
    x = x.reshape(n, D_MODEL)
    ks, vs, wkvs, shifts, convs, pools = [], [], [], [], [], []
    for l, p in enumerate(layers):
        x = _ffn(x, p["g"][0], p["g"][1], p["wg"][0], p["wu"][0], p["wd"][0])
        y_pool, u_rwkv, y_conv, q, k, v, pool_new, conv_new = _inproj(
            x, tlen, p["g"][2], p["w_in"], pool0[l], conv0[l], p["w_bd"], p["pool_scale"], p["conv_w"], past)
        seq = lambda a: a.reshape(bsz, tlen, -1)
        u_rwkv, q, k, v = map(seq, (u_rwkv, q, k, v))

        y_rwkv, wkv_new = _rwkv_mixer(u_rwkv, shift0[l], wkv0[l], p)

        if past:
            with_past = lambda old, new: _pad_time(
                jnp.concatenate([old[l].reshape(bsz, past, GROUP_W), new], axis=1), ATTN_TILE)
            y_sb = _sb_attn(q, with_past(k_past, k), with_past(v_past, v), past)
        else:
            y_sb = _sb_attn(q, k, v, past)

        flat = lambda a: a.reshape(n, GROUP_W)
        x = _ffn(x, p["g"][4], p["g"][5], p["wg"][1], p["wu"][1], p["wd"][1],
                 mix=(y_pool, flat(y_rwkv), y_conv, flat(y_sb), p["w_out"], p["g"][3]))

        ks.append(k.reshape(bsz, tlen, N_HEADS, HEAD_DIM))
        vs.append(v.reshape(bsz, tlen, N_HEADS, HEAD_DIM))
        wkvs.append(wkv_new)
        shifts.append(u_rwkv[:, -1])
        convs.append(conv_new)
        pools.append(pool_new[:, _POOL_PAD - POOL_STATE:])
    return (x.reshape(bsz, tlen, D_MODEL), jnp.stack(ks), jnp.stack(vs), jnp.stack(wkvs),
            jnp.stack(shifts), jnp.stack(convs), jnp.stack(pools))


def kernel(x_prompt, x_sample, cache_sb_k, cache_sb_v, state_wkv, state_shift, state_conv, state_pool,
           norm_g, ffn_w_gate, ffn_w_up, ffn_w_down, w_in, w_out, pool_w, pool_scale,
           rwkv_mu, rwkv_w0, rwkv_w2, rwkv_a0, rwkv_a2, rwkv_g2, rwkv_k_k, rwkv_k_a, rwkv_r_k,
           rwkv_ln_w, rwkv_ln_b, conv_w):
    layers = _prep_params(dict(
        norm_g=norm_g, ffn_w_gate=ffn_w_gate, ffn_w_up=ffn_w_up, ffn_w_down=ffn_w_down, w_in=w_in,
        w_out=w_out, pool_w=pool_w, pool_scale=pool_scale, rwkv_mu=rwkv_mu, rwkv_w0=rwkv_w0,
        rwkv_w2=rwkv_w2, rwkv_a0=rwkv_a0, rwkv_a2=rwkv_a2, rwkv_g2=rwkv_g2, rwkv_k_k=rwkv_k_k,
        rwkv_k_a=rwkv_k_a, rwkv_r_k=rwkv_r_k, rwkv_ln_w=rwkv_ln_w, rwkv_ln_b=rwkv_ln_b, conv_w=conv_w))
    depth = w_in.shape[0]
    bsz = x_prompt.shape[0]
    z_wkv = jnp.zeros((depth, bsz, N_HEADS, HEAD_DIM, HEAD_DIM), F32)
    z_shift = jnp.zeros((depth, bsz, RWKV_IN), F32)
    z_conv = jnp.zeros((depth, bsz, CONV_W - 1, GROUP_W), F32)
    z_pool = jnp.zeros((depth, bsz, POOL_STATE, GROUP_W), F32)
    prompt = _trunk(x_prompt, None, None, z_wkv, z_shift, z_conv, z_pool, layers, 0)
    sample = _trunk(x_sample, cache_sb_k, cache_sb_v, state_wkv, state_shift, state_conv, state_pool,
                    layers, cache_sb_k.shape[2])
    return (prompt[0], sample[0]) + prompt[1:] + sample[1:]
```

```python
import functools

import jax
import jax.numpy as jnp
from jax import lax
from jax.experimental import pallas as pl
from jax.experimental.pallas import tpu as pltpu

F32 = jnp.float32
BF16 = jnp.bfloat16

D_MODEL = 1024
GROUP_W = 256
HEAD_DIM = 64
N_HEADS = 4
POOL_WINDOWS = (2, 4, 8, 16)
POOL_STATE = 15
CONV_W = 3
RWKV_RANKS = (32, 32, 64)
RWKV_LOWRANK_W = sum(RWKV_RANKS)
RWKV_IN = 3 * GROUP_W + RWKV_LOWRANK_W
D_FF = 2816
RMS_EPS = 1e-6
GN_EPS = 64e-5
LOG2_E = 1.4426950408889634

FF_CHUNK = 256
TOKEN_TILE = 512
SCAN_CHUNK = 64
SCAN_BLOCK = 16
PACK_W = 256
SCAN_BATCH = 8
INV_PASSES = 3
STATE_PASSES = 3
ATTN_TILE = 256
ATTN_UNROLL = 4
VMEM_LIMIT = 56 * 1024 * 1024


def _cparams(sem):
    return pltpu.CompilerParams(dimension_semantics=sem, vmem_limit_bytes=VMEM_LIMIT)


def _rms(x, g):
    return x * lax.rsqrt(jnp.mean(x * x, axis=-1, keepdims=True) + RMS_EPS) * g


def _dot(a, b):
    return jnp.dot(a.astype(BF16), b.astype(BF16), preferred_element_type=F32)


def _dot_nt(a, b):
    return lax.dot_general(a, b, (((1,), (1,)), ((), ())), preferred_element_type=F32)


def _split2(a):
    hi = a.astype(BF16)
    return hi, (a - hi.astype(F32)).astype(BF16)


def _dot_split(a, ones_b):
    hi, lo = _split2(a)
    return (jnp.dot(hi, ones_b, preferred_element_type=F32)
            + jnp.dot(lo, ones_b, preferred_element_type=F32))


def _full(shape):
    return pl.BlockSpec(shape, lambda *_: (0,) * len(shape))


def _resident(shape):
    return pl.BlockSpec(shape, lambda *_: (0,) * len(shape), pipeline_mode=pl.Buffered(1))


def _ffn_body(mixed, x_ref, *refs):
    if mixed:
        mix_refs, wo_ref, gmix_ref = refs[:4], refs[4], refs[5]
        refs = refs[6:]
        y = sum(jnp.dot(m_ref[...], wo_ref[i * GROUP_W:(i + 1) * GROUP_W, :], preferred_element_type=F32)
                for i, m_ref in enumerate(mix_refs))
        x = x_ref[...] + _rms(y, gmix_ref[...])
    else:
        x = x_ref[...]
    gpre_ref, gpost_ref, wg_ref, wu_ref, wd_ref, o_ref, a_ref = refs
    hb = _rms(x, gpre_ref[...]).astype(BF16)
    for j in range(D_FF // FF_CHUNK):
        sl = slice(j * FF_CHUNK, (j + 1) * FF_CHUNK)
        g = jnp.dot(hb, wg_ref[:, sl], preferred_element_type=F32)
        u = jnp.dot(hb, wu_ref[:, sl], preferred_element_type=F32)
        a_ref[:, sl] = (g * jax.nn.sigmoid(g) * u).astype(BF16)
    y = jnp.dot(a_ref[...], wd_ref[...], preferred_element_type=F32)
    o_ref[...] = x + 0.5 * _rms(y, gpost_ref[...])


def _ffn(x, g_pre, g_post, wg, wu, wd, mix=None):
    n = x.shape[0]
    tm = min(TOKEN_TILE, n)
    row = pl.BlockSpec((tm, D_MODEL), lambda i: (i, 0))
    grp = pl.BlockSpec((tm, GROUP_W), lambda i: (i, 0))
    mix_specs = [] if mix is None else [grp] * 4 + [_resident((D_MODEL, D_MODEL)), _full((1, D_MODEL))]
    return pl.pallas_call(
        functools.partial(_ffn_body, mix is not None),
        grid=(n // tm,),
        in_specs=[row] + mix_specs + [_full((1, D_MODEL)), _full((1, D_MODEL)), _resident((D_MODEL, D_FF)),
                                      _resident((D_MODEL, D_FF)), _resident((D_FF, D_MODEL))],
        out_specs=row,
        out_shape=jax.ShapeDtypeStruct((n, D_MODEL), F32),
        scratch_shapes=[pltpu.VMEM((tm, D_FF), BF16)],
        compiler_params=_cparams(("arbitrary",)),
        name="ffn",
    )(x, *([] if mix is None else mix), g_pre, g_post, wg, wu, wd)


_PROJ_WIDTHS = (GROUP_W, RWKV_IN, 3 * GROUP_W, GROUP_W, GROUP_W, GROUP_W)
_PROJ_EDGES = tuple(sum(_PROJ_WIDTHS[:i]) for i in range(len(_PROJ_WIDTHS) + 1))
_POOL_PAD = 16
_ROW_PAD = 8


def _inproj_body(past, tiles_per_seq, x_ref, g_ref, w_ref, pst_ref, cst_ref, wbd_ref, scale_ref, cw_ref,
                 yp_ref, ur_ref, yc_ref, q_ref, k_ref, v_ref, pso_ref, cso_ref, pprev_ref, cprev_ref):
    t = pl.program_id(0) % tiles_per_seq
    tm = x_ref.shape[0]

    @pl.when(t == 0)
    def _():
        pprev_ref[...] = pst_ref[0]
        cprev_ref[...] = cst_ref[0]

    hb = _rms(x_ref[...], g_ref[...]).astype(BF16)
    proj = lambda i: jnp.dot(hb, w_ref[:, _PROJ_EDGES[i]:_PROJ_EDGES[i + 1]], preferred_element_type=F32)
    ur_ref[...] = proj(1)
    q_ref[...] = proj(3)
    k_ref[...] = proj(4)
    v_ref[...] = proj(5)

    u = proj(0)
    ext = jnp.concatenate([pprev_ref[...], u], axis=0)
    s2 = ext + pltpu.roll(ext, 1, 0)
    s4 = s2 + pltpu.roll(s2, 2, 0)
    s8 = s4 + pltpu.roll(s4, 4, 0)
    s16 = s8 + pltpu.roll(s8, 8, 0)
    grp = lax.broadcasted_iota(jnp.int32, (tm, GROUP_W), 1) // (GROUP_W // len(POOL_WINDOWS))
    pos1 = past + t * tm + lax.broadcasted_iota(jnp.int32, (tm, GROUP_W), 0) + 1
    win = jnp.where(grp == 0, s2[_POOL_PAD:], jnp.where(grp == 1, s4[_POOL_PAD:],
                    jnp.where(grp == 2, s8[_POOL_PAD:], s16[_POOL_PAD:])))
    width = jnp.where(grp == 0, 2, jnp.where(grp == 1, 4, jnp.where(grp == 2, 8, 16)))
    cnt = jnp.minimum(width, pos1).astype(F32)
    d = win / cnt - u
    yp_ref[...] = (_dot(d, wbd_ref[...]) * scale_ref[...]).astype(BF16)
    pprev_ref[...] = ext[tm:]
    pso_ref[0] = ext[tm:]

    uc = proj(2)
    b = uc[:, 0:GROUP_W]
    z = uc[:, GROUP_W:2 * GROUP_W] * uc[:, 2 * GROUP_W:3 * GROUP_W]
    extz = jnp.concatenate([cprev_ref[...], z], axis=0)
    y = (pltpu.roll(extz, 2, 0)[_ROW_PAD:] * cw_ref[0:1, :]
         + pltpu.roll(extz, 1, 0)[_ROW_PAD:] * cw_ref[1:2, :]
         + z * cw_ref[2:3, :])
    yc_ref[...] = (b * y).astype(BF16)
    cprev_ref[...] = extz[tm:]
    cso_ref[0] = extz[tm:][_ROW_PAD - (CONV_W - 1):]


def _inproj(x, tlen, g, w_in, pool_state, conv_state, w_bd, scale, conv_w, past):
    n = x.shape[0]
    bsz = n // tlen
    tm = min(TOKEN_TILE, tlen)
    tiles_per_seq = tlen // tm
    pst = jnp.pad(pool_state, ((0, 0), (_POOL_PAD - POOL_STATE, 0), (0, 0)))
    cst = jnp.pad(conv_state, ((0, 0), (_ROW_PAD - (CONV_W - 1), 0), (0, 0)))
    row = lambda w: pl.BlockSpec((tm, w), lambda i: (i, 0))
    per_seq = lambda r: pl.BlockSpec((1, r, GROUP_W), lambda i: (i // tiles_per_seq, 0, 0))
    flat = lambda w, dt: jax.ShapeDtypeStruct((n, w), dt)
    return pl.pallas_call(
        functools.partial(_inproj_body, past, tiles_per_seq),
        grid=(n // tm,),
        in_specs=[row(D_MODEL), _full((1, D_MODEL)), _resident((D_MODEL, _PROJ_EDGES[-1])),
                  per_seq(_POOL_PAD), per_seq(_ROW_PAD),
                  _full((GROUP_W, GROUP_W)), _full((1, GROUP_W)), _full((CONV_W, GROUP_W))],
        out_specs=[row(GROUP_W), row(RWKV_IN), row(GROUP_W), row(GROUP_W), row(GROUP_W), row(GROUP_W),
                   per_seq(_POOL_PAD), per_seq(CONV_W - 1)],
        out_shape=[flat(GROUP_W, BF16), flat(RWKV_IN, F32), flat(GROUP_W, BF16), flat(GROUP_W, F32),
                   flat(GROUP_W, F32), flat(GROUP_W, F32),
                   jax.ShapeDtypeStruct((bsz, _POOL_PAD, GROUP_W), F32),
                   jax.ShapeDtypeStruct((bsz, CONV_W - 1, GROUP_W), F32)],
        scratch_shapes=[pltpu.VMEM((_POOL_PAD, GROUP_W), F32), pltpu.VMEM((_ROW_PAD, GROUP_W), F32)],
        compiler_params=_cparams(("arbitrary",)),
        name="inproj",
    )(x, g, w_in, pst, cst, w_bd, scale, conv_w)


def _softplus(x):
    return jnp.maximum(x, 0.0) + jnp.log1p(jnp.exp(-jnp.abs(x)))


_NN = ((2,), (1,))
_NT = ((2,), (2,))
_TN = ((1,), (1,))


def _bmm(a, b, dims=_NN, passes=1):
    dg = lambda x, y: lax.dot_general(x, y, (dims, ((0,), (0,))), preferred_element_type=F32)
    if passes == 1:
        return dg(a.astype(BF16), b.astype(BF16))
    ah, al = _split2(a)
    bh, bl = _split2(b)
    if dims == _TN:
        return dg(ah, bh) + (dg(ah, bl) + dg(al, bh))
    m = a.shape[1]
    both = dg(jnp.concatenate([ah, al], axis=1), bh)
    return both[:, :m] + (both[:, m:] + dg(ah, bl))


def _series_inverse(m, eye, order, mm):
    x = eye - m
    p = m
    n = 2
    while n < order:
        p = mm(p, p)
        x = x + mm(x, p)
        n *= 2
    return x


def _diag_block_inverse(a_diag, same_block, c):
    n = a_diag.shape[0]
    per_row = PACK_W // c
    folded = sum(a_diag[:, i * SCAN_BLOCK:(i + 1) * SCAN_BLOCK, :] for i in range(c // SCAN_BLOCK))
    packed = jnp.stack([jnp.concatenate([folded[g * per_row + j] for j in range(per_row)], axis=-1)
                        for g in range(n // per_row)])
    lane = lax.broadcasted_iota(jnp.int32, (SCAN_BLOCK, PACK_W), 1)
    eye = (lax.broadcasted_iota(jnp.int32, (SCAN_BLOCK, PACK_W), 0) == lane % SCAN_BLOCK).astype(F32)
    on_diag = ((lax.broadcasted_iota(jnp.int32, (PACK_W, PACK_W), 0) // SCAN_BLOCK)
               == (lax.broadcasted_iota(jnp.int32, (PACK_W, PACK_W), 1) // SCAN_BLOCK)).astype(BF16)

    def spread(y):
        return jnp.concatenate([y] * (PACK_W // SCAN_BLOCK), axis=1) * on_diag

    def mm(x, y):
        dg = lambda p, q: lax.dot_general(p, q, (_NN, ((0,), (0,))), preferred_element_type=F32)
        if INV_PASSES == 1:
            return dg(x.astype(BF16), spread(y.astype(BF16)))
        xh, xl = _split2(x)
        yh, yl = _split2(y)
        yh_d = spread(yh)
        return dg(xh, yh_d) + (dg(xh, spread(yl)) + dg(xl, yh_d))

    x = _series_inverse(packed, eye, SCAN_BLOCK, mm)
    blocks = [x[g][:, j * c:(j + 1) * c] for g in range(n // per_row) for j in range(per_row)]
    return jnp.stack([jnp.where(same_block, jnp.concatenate([blk] * (c // SCAN_BLOCK), axis=0), 0.0)
                      for blk in blocks])


def _unit_lower_inverse(a, c):
    row = lax.broadcasted_iota(jnp.int32, (c, c), 0)
    col = lax.broadcasted_iota(jnp.int32, (c, c), 1)
    eye = (row == col).astype(F32)
    same_block = (row // SCAN_BLOCK) == (col // SCAN_BLOCK)
    a_diag = jnp.where(same_block, a, 0.0)
    t_diag = _diag_block_inverse(a_diag, same_block, c)
    mm = functools.partial(_bmm, passes=INV_PASSES)
    b = mm(t_diag, a - a_diag)
    return mm(_series_inverse(b, eye, c // SCAN_BLOCK, mm), t_diag)


def _scan_body(valid, u_ref, sh_ref, mu_ref, w0_ref, w2_ref, a0_ref, a2_ref, g2_ref, kk_ref, ka_ref, hsum_ref,
               s0_ref, rk_ref, lnw_ref, lnb_ref, y_ref, so_ref, s_ref, prev_ref):
    t = pl.program_id(1)
    nb, c, _ = u_ref.shape

    @pl.when(t == 0)
    def _():
        s_ref[...] = s0_ref[...]
        prev_ref[...] = sh_ref[...]

    shifted = []
    for i in range(nb):
        u = u_ref[i]
        ext = jnp.concatenate([prev_ref[i], u], axis=0)
        prev_ref[i] = ext[c:]
        shifted.append(u + (pltpu.roll(ext, 1, 0)[_ROW_PAD:] - u) * mu_ref[...])
    xs = jnp.concatenate(shifted, axis=0)
    low = xs[:, 3 * GROUP_W:]
    w = -_softplus(-(w0_ref[...] + _dot(jnp.tanh(low), w2_ref[...]))) - 0.5
    a = jax.nn.sigmoid(a0_ref[...] + _dot(low, a2_ref[...]))
    k_raw = xs[:, GROUP_W:2 * GROUP_W]
    kk = k_raw * kk_ref[...]
    norm = jnp.sqrt(_dot_split(kk * kk, hsum_ref[...]))
    kap = kk / jnp.maximum(norm, 1e-12)
    steps = [xs[:, 0:GROUP_W],
             -jnp.exp(w),
             k_raw * (1.0 + (a - 1.0) * ka_ref[...]),
             xs[:, 2 * GROUP_W:3 * GROUP_W],
             kap,
             kap * a]
    if valid < c:
        live = lax.broadcasted_iota(jnp.int32, (nb * c, GROUP_W), 0) % c < valid
        steps = [jnp.where(live, x, 0.0) for x in steps]
    r_all, lw, k_all, v_all, kap_all, b_all = (x.reshape(nb, c, GROUP_W) for x in steps)
    gate = _dot(jax.nn.sigmoid(low), g2_ref[...]).reshape(nb, c, GROUP_W)

    row = lax.broadcasted_iota(jnp.int32, (c, c), 0)
    col = lax.broadcasted_iota(jnp.int32, (c, c), 1)
    incl = row >= col
    strict = row > col
    tri = incl.astype(BF16)

    def heads(x):
        return jnp.stack([x[i][:, h * HEAD_DIM:(h + 1) * HEAD_DIM]
                          for i in range(nb) for h in range(N_HEADS)])

    lw_hi, lw_lo = _split2(lw)
    lw_lo2 = (lw - lw_hi.astype(F32) - lw_lo.astype(F32)).astype(BF16)
    cum = jnp.stack([jnp.dot(tri, lw_hi[i], preferred_element_type=F32)
                     + (jnp.dot(tri, lw_lo[i], preferred_element_type=F32)
                        + jnp.dot(tri, lw_lo2[i], preferred_element_type=F32)) for i in range(nb)])
    cmid = cum[:, c // 2 - 1:c // 2, :]
    cend = cum[:, c - 1:c, :]
    e_bwd = jnp.exp(cmid - cum)
    e_end = jnp.exp(cend - cum)
    r_t = heads(r_all * jnp.exp(cum - cmid))
    kap_t = heads(kap_all * jnp.exp(cum - lw - cmid))
    k_t = heads(k_all * e_bwd)
    b_t = heads(b_all * e_bwd)
    r_0 = heads(r_all * jnp.exp(cum))
    kap_0 = heads(kap_all * jnp.exp(cum - lw))
    k_e = heads(k_all * e_end)
    b_e = heads(b_all * e_end)
    g_end = heads(jnp.exp(cend))
    v = heads(v_all)
    s = s_ref[...].reshape(nb * N_HEADS, HEAD_DIM, HEAD_DIM)

    stack = lambda top, bottom: jnp.concatenate([top, bottom], axis=1)
    with_b = _bmm(stack(kap_t, r_t), b_t, _NT)
    with_k = _bmm(stack(kap_t, r_t), k_t, _NT)
    a_kb = jnp.where(strict, with_b[:, :c], 0.0)
    a_rb = jnp.where(incl, with_b[:, c:], 0.0)
    a_kk = jnp.where(strict, with_k[:, :c], 0.0)
    a_rk = jnp.where(incl, with_k[:, c:], 0.0)
    t_inv = _unit_lower_inverse(a_kb, c)

    from_s = _bmm(stack(kap_0, r_0), s, _NT, STATE_PASSES)
    from_v = _bmm(stack(a_kk, a_rk), v)
    u = _bmm(t_inv, from_s[:, :c] + from_v[:, :c], passes=STATE_PASSES)
    y = from_s[:, c:] + from_v[:, c:] - _bmm(a_rb, u)
    s_new = s * g_end + _bmm(stack(v, u), stack(k_e, -b_e), _TN, STATE_PASSES)
    s_new = s_new.reshape(nb, N_HEADS, HEAD_DIM, HEAD_DIM)
    s_ref[...] = s_new
    so_ref[...] = s_new

    mean = jnp.mean(y, axis=-1, keepdims=True)
    yc = y - mean
    var = jnp.mean(yc * yc, axis=-1, keepdims=True)
    yn = yc * lax.rsqrt(var + GN_EPS)
    bonus = jnp.sum(heads(r_all * k_all * rk_ref[...]), axis=-1, keepdims=True) * v
    for i in range(nb):
        wide = lambda x: jnp.concatenate([x[i * N_HEADS + h] for h in range(N_HEADS)], axis=-1)
        y_ref[i] = ((wide(yn) * lnw_ref[...] + lnb_ref[...] + wide(bonus)) * gate[i]).astype(BF16)


def _rwkv_mixer(u, shift_state, s0, p):
    bsz, tlen, _ = u.shape
    c = SCAN_CHUNK
    nb = SCAN_BATCH
    u = _pad_time(u, c)
    sh = jnp.pad(shift_state[:, None, :], ((0, 0), (_ROW_PAD - 1, 0), (0, 0)))
    st = pl.BlockSpec((nb, N_HEADS, HEAD_DIM, HEAD_DIM), lambda b_, t: (b_, 0, 0, 0))
    vec = _full((1, GROUP_W))
    low = _full((RWKV_LOWRANK_W, GROUP_W))
    y, state = pl.pallas_call(
        functools.partial(_scan_body, min(tlen, c)),
        grid=(bsz // nb, u.shape[1] // c),
        in_specs=[pl.BlockSpec((nb, c, RWKV_IN), lambda b_, t: (b_, t, 0)),
                  pl.BlockSpec((nb, _ROW_PAD, RWKV_IN), lambda b_, t: (b_, 0, 0)),
                  _full((1, RWKV_IN)), vec, low, vec, low, low, vec, vec, _full((GROUP_W, GROUP_W)),
                  st, vec, vec, vec],
        out_specs=[pl.BlockSpec((nb, c, GROUP_W), lambda b_, t: (b_, t, 0)), st],
        out_shape=[jax.ShapeDtypeStruct((bsz, u.shape[1], GROUP_W), BF16),
                   jax.ShapeDtypeStruct((bsz, N_HEADS, HEAD_DIM, HEAD_DIM), F32)],
        scratch_shapes=[pltpu.VMEM((nb, N_HEADS, HEAD_DIM, HEAD_DIM), F32),
                        pltpu.VMEM((nb, _ROW_PAD, RWKV_IN), F32)],
        compiler_params=_cparams(("arbitrary", "arbitrary")),
        name="rwkv_mixer",
    )(u, sh, p["mu"], p["w0"], p["w2p"], p["a0"], p["a2p"], p["g2p"], p["k_k"], p["k_a"], p["hsum"],
      s0, p["r_k"], p["ln_w"], p["ln_b"])
    return y[:, :tlen], state


def _attn_body(past, q_ref, k_ref, v_ref, o_ref, kb_ref, vb_ref):
    qi = pl.program_id(1)
    tq = q_ref.shape[1]
    tk = ATTN_TILE
    q0 = past + qi * tq
    diag = q0 // tk
    scale = HEAD_DIM ** -0.5 * LOG2_E

    @pl.when(qi == 0)
    def _():
        kb_ref[...] = k_ref[0].astype(BF16)
        vb_ref[...] = v_ref[0].astype(BF16)

    heads = [slice(h * HEAD_DIM, (h + 1) * HEAD_DIM) for h in range(N_HEADS)]
    qs = [(q_ref[0, :, sl] * scale).astype(BF16) for sl in heads]
    later = (lax.broadcasted_iota(jnp.int32, (tk, tk), 0)
             > lax.broadcasted_iota(jnp.int32, (tk, tk), 1)).astype(BF16)
    every = range(N_HEADS)

    def tiles(js, carry, masked):
        accs, runs = carry
        units = [(t, h) for t in range(len(js)) for h in every]
        kt = [kb_ref[pl.ds(pl.multiple_of(j * tk, tk), tk), :] for j in js]
        vt = [vb_ref[pl.ds(pl.multiple_of(j * tk, tk), tk), :] for j in js]
        if masked:
            vis = [(j * tk + lax.broadcasted_iota(jnp.int32, (tq, tk), 1)
                    < q0 + lax.broadcasted_iota(jnp.int32, (tq, tk), 0)) for j in js]
        z = [_dot_nt(qs[h], kt[t][:, heads[h]]) for t, h in units]
        log_take = [jnp.minimum(x, 0.0) - jnp.log2(1.0 + jnp.exp2(-jnp.abs(x))) for x in z]
        log_keep = [a - b for a, b in zip(log_take, z)]
        if masked:
            log_keep = [jnp.where(vis[t], x, 0.0) for (t, _), x in zip(units, log_keep)]
        after = [_dot(x, later) for x in log_keep]
        p = [jnp.exp2(a + b) for a, b in zip(log_take, after)]
        if masked:
            p = [jnp.where(vis[t], x, 0.0) for (t, _), x in zip(units, p)]
        part = [jnp.dot(x.astype(BF16), vt[t][:, heads[h]], preferred_element_type=F32)
                for (t, h), x in zip(units, p)]
        accs, runs = list(accs), list(runs)
        for n, (_, h) in enumerate(units):
            accs[h] = accs[h] + jnp.exp2(runs[h]) * part[n]
            runs[h] = runs[h] + (after[n][:, 0:1] + log_keep[n][:, 0:1])
        return tuple(accs), tuple(runs)

    carry = (tuple(jnp.zeros((tq, HEAD_DIM), F32) for _ in every),
             tuple(jnp.zeros((tq, 1), F32) for _ in every))
    carry = tiles([diag], carry, True)
    single = diag % ATTN_UNROLL
    carry = lax.fori_loop(0, single, lambda i, c: tiles([diag - 1 - i], c, False), carry)
    first = diag - single - 1
    accs, _ = lax.fori_loop(
        0, diag // ATTN_UNROLL,
        lambda i, c: tiles([first - ATTN_UNROLL * i - t for t in range(ATTN_UNROLL)], c, False), carry)
    o_ref[0] = jnp.concatenate(accs, axis=-1).astype(BF16)


def _sb_attn(q, k_all, v_all, past):
    bsz, tlen, _ = q.shape
    tq = min(ATTN_TILE, tlen)
    slen = k_all.shape[1]
    kv = pl.BlockSpec((1, slen, GROUP_W), lambda b, i: (b, 0, 0))
    return pl.pallas_call(
        functools.partial(_attn_body, past),
        grid=(bsz, tlen // tq),
        in_specs=[pl.BlockSpec((1, tq, GROUP_W), lambda b, i: (b, i, 0)), kv, kv],
        out_specs=pl.BlockSpec((1, tq, GROUP_W), lambda b, i: (b, i, 0)),
        out_shape=jax.ShapeDtypeStruct((bsz, tlen, GROUP_W), BF16),
        scratch_shapes=[pltpu.VMEM((slen, GROUP_W), BF16), pltpu.VMEM((slen, GROUP_W), BF16)],
        compiler_params=_cparams(("arbitrary", "arbitrary")),
        name="sb_attn",
    )(q, k_all, v_all)


def _pad_time(x, mult):
    pad = (-x.shape[1]) % mult
    return x if pad == 0 else jnp.pad(x, ((0, 0), (0, pad), (0, 0)))


def _prep_params(prm):
    out = []
    depth = prm["w_in"].shape[0]
    head_of = jnp.arange(GROUP_W) // HEAD_DIM
    hsum = (head_of[:, None] == head_of[None, :]).astype(BF16)
    row2 = lambda v: v.reshape(1, -1)
    for l in range(depth):
        w_bd = jax.scipy.linalg.block_diag(*[prm["pool_w"][l, i] for i in range(len(POOL_WINDOWS))])
        lowpad = lambda w, lo: jnp.zeros((RWKV_LOWRANK_W, GROUP_W), F32).at[lo:lo + w.shape[0]].set(w)
        low_at = [sum(RWKV_RANKS[:i]) for i in range(len(RWKV_RANKS))]
        out.append(dict(
            g=[row2(prm["norm_g"][l, i]) for i in range(6)],
            wg=[prm["ffn_w_gate"][l, i].astype(BF16) for i in range(2)],
            wu=[prm["ffn_w_up"][l, i].astype(BF16) for i in range(2)],
            wd=[prm["ffn_w_down"][l, i].astype(BF16) for i in range(2)],
            w_in=prm["w_in"][l].astype(BF16),
            w_out=prm["w_out"][l].astype(BF16),
            w_bd=w_bd.astype(BF16),
            pool_scale=row2(prm["pool_scale"][l]),
            conv_w=prm["conv_w"][l],
            mu=row2(prm["rwkv_mu"][l]),
            w0=row2(prm["rwkv_w0"][l]),
            w2p=lowpad(prm["rwkv_w2"][l], low_at[0]).astype(BF16),
            a0=row2(prm["rwkv_a0"][l]),
            a2p=lowpad(prm["rwkv_a2"][l], low_at[1]).astype(BF16),
            g2p=lowpad(prm["rwkv_g2"][l], low_at[2]).astype(BF16),
            k_k=row2(prm["rwkv_k_k"][l]),
            k_a=row2(prm["rwkv_k_a"][l]),
            r_k=row2(prm["rwkv_r_k"][l]),
            ln_w=row2(prm["rwkv_ln_w"][l]),
            ln_b=row2(prm["rwkv_ln_b"][l]),
            hsum=hsum,
        ))
    return out


def _trunk(x, k_past, v_past, wkv0, shift0, conv0, pool0, layers, past):
    bsz, tlen, _ = x.shape
    n = bsz * tlen
    x = x.reshape(n, D_MODEL)
    ks, vs, wkvs, shifts, convs, pools = [], [], [], [], [], []
    for l, p in enumerate(layers):
        x = _ffn(x, p["g"][0], p["g"][1], p["wg"][0], p["wu"][0], p["wd"][0])
        y_pool, u_rwkv, y_conv, q, k, v, pool_new, conv_new = _inproj(
            x, tlen, p["g"][2], p["w_in"], pool0[l], conv0[l], p["w_bd"], p["pool_scale"], p["conv_w"], past)
        seq = lambda a: a.reshape(bsz, tlen, -1)
        u_rwkv, q, k, v = map(seq, (u_rwkv, q, k, v))

        y_rwkv, wkv_new = _rwkv_mixer(u_rwkv, shift0[l], wkv0[l], p)

        if past:
            with_past = lambda old, new: _pad_time(
                jnp.concatenate([old[l].reshape(bsz, past, GROUP_W), new], axis=1), ATTN_TILE)
            y_sb = _sb_attn(q, with_past(k_past, k), with_past(v_past, v), past)
        else:
            y_sb = _sb_attn(q, k, v, past)

        flat = lambda a: a.reshape(n, GROUP_W)
        x = _ffn(x, p["g"][4], p["g"][5], p["wg"][1], p["wu"][1], p["wd"][1],
                 mix=(y_pool, flat(y_rwkv), y_conv, flat(y_sb), p["w_out"], p["g"][3]))

        ks.append(k.reshape(bsz, tlen, N_HEADS, HEAD_DIM))
        vs.append(v.reshape(bsz, tlen, N_HEADS, HEAD_DIM))
        wkvs.append(wkv_new)
        shifts.append(u_rwkv[:, -1])
        convs.append(conv_new)
        pools.append(pool_new[:, _POOL_PAD - POOL_STATE:])
    return (x.reshape(bsz, tlen, D_MODEL), jnp.stack(ks), jnp.stack(vs), jnp.stack(wkvs),
            jnp.stack(shifts), jnp.stack(convs), jnp.stack(pools))


def kernel(x_prompt, x_sample, cache_sb_k, cache_sb_v, state_wkv, state_shift, state_conv, state_pool,
           norm_g, ffn_w_gate, ffn_w_up, ffn_w_down, w_in, w_out, pool_w, pool_scale,
           rwkv_mu, rwkv_w0, rwkv_w2, rwkv_a0, rwkv_a2, rwkv_g2, rwkv_k_k, rwkv_k_a, rwkv_r_k,
           rwkv_ln_w, rwkv_ln_b, conv_w):
    layers = _prep_params(dict(
        norm_g=norm_g, ffn_w_gate=ffn_w_gate, ffn_w_up=ffn_w_up, ffn_w_down=ffn_w_down, w_in=w_in,
        w_out=w_out, pool_w=pool_w, pool_scale=pool_scale, rwkv_mu=rwkv_mu, rwkv_w0=rwkv_w0,
        rwkv_w2=rwkv_w2, rwkv_a0=rwkv_a0, rwkv_a2=rwkv_a2, rwkv_g2=rwkv_g2, rwkv_k_k=rwkv_k_k,
        rwkv_k_a=rwkv_k_a, rwkv_r_k=rwkv_r_k, rwkv_ln_w=rwkv_ln_w, rwkv_ln_b=rwkv_ln_b, conv_w=conv_w))
    depth = w_in.shape[0]
    bsz = x_prompt.shape[0]
    z_wkv = jnp.zeros((depth, bsz, N_HEADS, HEAD_DIM, HEAD_DIM), F32)
    z_shift = jnp.zeros((depth, bsz, RWKV_IN), F32)
    z_conv = jnp.zeros((depth, bsz, CONV_W - 1, GROUP_W), F32)
    z_pool = jnp.zeros((depth, bsz, POOL_STATE, GROUP_W), F32)
    prompt = _trunk(x_prompt, None, None, z_wkv, z_shift, z_conv, z_pool, layers, 0)
    sample = _trunk(x_sample, cache_sb_k, cache_sb_v, state_wkv, state_shift, state_conv, state_pool,
                    layers, cache_sb_k.shape[2])
    return (prompt[0], sample[0]) + prompt[1:] + sample[1:]
```

```python
import functools

import jax
import jax.numpy as jnp
from jax import lax
from jax.experimental import pallas as pl
from jax.experimental.pallas import tpu as pltpu

F32 = jnp.float32
BF16 = jnp.bfloat16

D_MODEL = 1024
GROUP_W = 256
HEAD_DIM = 64
N_HEADS = 4
POOL_WINDOWS = (2, 4, 8, 16)
POOL_STATE = 15
CONV_W = 3
RWKV_RANKS = (32, 32, 64)
RWKV_LOWRANK_W = sum(RWKV_RANKS)
RWKV_IN = 3 * GROUP_W + RWKV_LOWRANK_W
D_FF = 2816
RMS_EPS = 1e-6
GN_EPS = 64e-5
LOG2_E = 1.4426950408889634

FF_CHUNK = 256
TOKEN_TILE = 512
PROJ_TILE = 1024
SCAN_CHUNK = 64
SCAN_BLOCK = 16
PACK_W = 256
SCAN_BATCH = 8
INV_PASSES = 3
STATE_PASSES = 3
ATTN_TILE = 256
ATTN_UNROLL = 4
VMEM_LIMIT = 56 * 1024 * 1024


def _cparams(sem):
    return pltpu.CompilerParams(dimension_semantics=sem, vmem_limit_bytes=VMEM_LIMIT)


def _rms(x, g):
    return x * lax.rsqrt(jnp.mean(x * x, axis=-1, keepdims=True) + RMS_EPS) * g


def _dot(a, b):
    return jnp.dot(a.astype(BF16), b.astype(BF16), preferred_element_type=F32)


def _dot_nt(a, b):
    return lax.dot_general(a, b, (((1,), (1,)), ((), ())), preferred_element_type=F32)


def _split2(a):
    hi = a.astype(BF16)
    return hi, (a - hi.astype(F32)).astype(BF16)


def _dot_split(a, ones_b):
    hi, lo = _split2(a)
    return (jnp.dot(hi, ones_b, preferred_element_type=F32)
            + jnp.dot(lo, ones_b, preferred_element_type=F32))


def _full(shape):
    return pl.BlockSpec(shape, lambda *_: (0,) * len(shape))


def _resident(shape):
    return pl.BlockSpec(shape, lambda *_: (0,) * len(shape), pipeline_mode=pl.Buffered(1))


def _ffn_body(mixed, x_ref, *refs):
    if mixed:
        mix_refs, wo_ref, gmix_ref = refs[:4], refs[4], refs[5]
        refs = refs[6:]
        y = sum(jnp.dot(m_ref[...], wo_ref[i * GROUP_W:(i + 1) * GROUP_W, :], preferred_element_type=F32)
                for i, m_ref in enumerate(mix_refs))
        x = x_ref[...] + _rms(y, gmix_ref[...])
    else:
        x = x_ref[...]
    gpre_ref, gpost_ref, wg_ref, wu_ref, wd_ref, o_ref, a_ref = refs
    hb = _rms(x, gpre_ref[...]).astype(BF16)
    for j in range(D_FF // FF_CHUNK):
        sl = slice(j * FF_CHUNK, (j + 1) * FF_CHUNK)
        g = jnp.dot(hb, wg_ref[:, sl], preferred_element_type=F32)
        u = jnp.dot(hb, wu_ref[:, sl], preferred_element_type=F32)
        a_ref[:, sl] = (g * jax.nn.sigmoid(g) * u).astype(BF16)
    y = jnp.dot(a_ref[...], wd_ref[...], preferred_element_type=F32)
    o_ref[...] = x + 0.5 * _rms(y, gpost_ref[...])


def _ffn(x, g_pre, g_post, wg, wu, wd, mix=None):
    n = x.shape[0]
    tm = min(TOKEN_TILE, n)
    row = pl.BlockSpec((tm, D_MODEL), lambda i: (i, 0))
    grp = pl.BlockSpec((tm, GROUP_W), lambda i: (i, 0))
    mix_specs = [] if mix is None else [grp] * 4 + [_resident((D_MODEL, D_MODEL)), _full((1, D_MODEL))]
    return pl.pallas_call(
        functools.partial(_ffn_body, mix is not None),
        grid=(n // tm,),
        in_specs=[row] + mix_specs + [_full((1, D_MODEL)), _full((1, D_MODEL)), _resident((D_MODEL, D_FF)),
                                      _resident((D_MODEL, D_FF)), _resident((D_FF, D_MODEL))],
        out_specs=row,
        out_shape=jax.ShapeDtypeStruct((n, D_MODEL), F32),
        scratch_shapes=[pltpu.VMEM((tm, D_FF), BF16)],
        compiler_params=_cparams(("arbitrary",)),
        name="ffn",
    )(x, *([] if mix is None else mix), g_pre, g_post, wg, wu, wd)


_PROJ_WIDTHS = (GROUP_W, RWKV_IN, 3 * GROUP_W, GROUP_W, GROUP_W, GROUP_W)
_PROJ_EDGES = tuple(sum(_PROJ_WIDTHS[:i]) for i in range(len(_PROJ_WIDTHS) + 1))
_POOL_PAD = 16
_ROW_PAD = 8


def _inproj_body(past, tiles_per_seq, x_ref, g_ref, w_ref, pst_ref, cst_ref, wbd_ref, scale_ref, cw_ref,
                 yp_ref, ur_ref, yc_ref, q_ref, k_ref, v_ref, pso_ref, cso_ref, pprev_ref, cprev_ref):
    t = pl.program_id(0) % tiles_per_seq
    tm = x_ref.shape[0]

    @pl.when(t == 0)
    def _():
        pprev_ref[...] = pst_ref[0]
        cprev_ref[...] = cst_ref[0]

    hb = _rms(x_ref[...], g_ref[...]).astype(BF16)
    proj = lambda i: jnp.dot(hb, w_ref[:, _PROJ_EDGES[i]:_PROJ_EDGES[i + 1]], preferred_element_type=F32)
    ur_ref[...] = proj(1)
    q_ref[...] = proj(3)
    k_ref[...] = proj(4)
    v_ref[...] = proj(5)

    u = proj(0)
    ext = jnp.concatenate([pprev_ref[...], u], axis=0)
    s2 = ext + pltpu.roll(ext, 1, 0)
    s4 = s2 + pltpu.roll(s2, 2, 0)
    s8 = s4 + pltpu.roll(s4, 4, 0)
    s16 = s8 + pltpu.roll(s8, 8, 0)
    grp = lax.broadcasted_iota(jnp.int32, (tm, GROUP_W), 1) // (GROUP_W // len(POOL_WINDOWS))
    pos1 = past + t * tm + lax.broadcasted_iota(jnp.int32, (tm, GROUP_W), 0) + 1
    win = jnp.where(grp == 0, s2[_POOL_PAD:], jnp.where(grp == 1, s4[_POOL_PAD:],
                    jnp.where(grp == 2, s8[_POOL_PAD:], s16[_POOL_PAD:])))
    width = jnp.where(grp == 0, 2, jnp.where(grp == 1, 4, jnp.where(grp == 2, 8, 16)))
    cnt = jnp.minimum(width, pos1).astype(F32)
    d = win / cnt - u
    yp_ref[...] = (_dot(d, wbd_ref[...]) * scale_ref[...]).astype(BF16)
    pprev_ref[...] = ext[tm:]
    pso_ref[0] = ext[tm:]

    uc = proj(2)
    b = uc[:, 0:GROUP_W]
    z = uc[:, GROUP_W:2 * GROUP_W] * uc[:, 2 * GROUP_W:3 * GROUP_W]
    extz = jnp.concatenate([cprev_ref[...], z], axis=0)
    y = (pltpu.roll(extz, 2, 0)[_ROW_PAD:] * cw_ref[0:1, :]
         + pltpu.roll(extz, 1, 0)[_ROW_PAD:] * cw_ref[1:2, :]
         + z * cw_ref[2:3, :])
    yc_ref[...] = (b * y).astype(BF16)
    cprev_ref[...] = extz[tm:]
    cso_ref[0] = extz[tm:][_ROW_PAD - (CONV_W - 1):]


def _inproj(x, tlen, g, w_in, pool_state, conv_state, w_bd, scale, conv_w, past):
    n = x.shape[0]
    bsz = n // tlen
    tm = min(PROJ_TILE, tlen)
    tiles_per_seq = tlen // tm
    pst = jnp.pad(pool_state, ((0, 0), (_POOL_PAD - POOL_STATE, 0), (0, 0)))
    cst = jnp.pad(conv_state, ((0, 0), (_ROW_PAD - (CONV_W - 1), 0), (0, 0)))
    row = lambda w: pl.BlockSpec((tm, w), lambda i: (i, 0))
    per_seq = lambda r: pl.BlockSpec((1, r, GROUP_W), lambda i: (i // tiles_per_seq, 0, 0))
    flat = lambda w, dt: jax.ShapeDtypeStruct((n, w), dt)
    return pl.pallas_call(
        functools.partial(_inproj_body, past, tiles_per_seq),
        grid=(n // tm,),
        in_specs=[row(D_MODEL), _full((1, D_MODEL)), _resident((D_MODEL, _PROJ_EDGES[-1])),
                  per_seq(_POOL_PAD), per_seq(_ROW_PAD),
                  _full((GROUP_W, GROUP_W)), _full((1, GROUP_W)), _full((CONV_W, GROUP_W))],
        out_specs=[row(GROUP_W), row(RWKV_IN), row(GROUP_W), row(GROUP_W), row(GROUP_W), row(GROUP_W),
                   per_seq(_POOL_PAD), per_seq(CONV_W - 1)],
        out_shape=[flat(GROUP_W, BF16), flat(RWKV_IN, F32), flat(GROUP_W, BF16), flat(GROUP_W, F32),
                   flat(GROUP_W, F32), flat(GROUP_W, F32),
                   jax.ShapeDtypeStruct((bsz, _POOL_PAD, GROUP_W), F32),
                   jax.ShapeDtypeStruct((bsz, CONV_W - 1, GROUP_W), F32)],
        scratch_shapes=[pltpu.VMEM((_POOL_PAD, GROUP_W), F32), pltpu.VMEM((_ROW_PAD, GROUP_W), F32)],
        compiler_params=_cparams(("arbitrary",)),
        name="inproj",
    )(x, g, w_in, pst, cst, w_bd, scale, conv_w)


def _softplus(x):
    return jnp.maximum(x, 0.0) + jnp.log1p(jnp.exp(-jnp.abs(x)))


_NN = ((2,), (1,))
_NT = ((2,), (2,))
_TN = ((1,), (1,))


def _bmm(a, b, dims=_NN, passes=1):
    dg = lambda x, y: lax.dot_general(x, y, (dims, ((0,), (0,))), preferred_element_type=F32)
    if passes == 1:
        return dg(a.astype(BF16), b.astype(BF16))
    ah, al = _split2(a)
    bh, bl = _split2(b)
    if dims == _TN:
        return dg(ah, bh) + (dg(ah, bl) + dg(al, bh))
    m = a.shape[1]
    both = dg(jnp.concatenate([ah, al], axis=1), bh)
    return both[:, :m] + (both[:, m:] + dg(ah, bl))


def _series_inverse(m, eye, order, mm):
    x = eye - m
    p = m
    n = 2
    while n < order:
        p = mm(p, p)
        x = x + mm(x, p)
        n *= 2
    return x


def _diag_block_inverse(a_diag, same_block, c):
    n = a_diag.shape[0]
    per_row = PACK_W // c
    folded = sum(a_diag[:, i * SCAN_BLOCK:(i + 1) * SCAN_BLOCK, :] for i in range(c // SCAN_BLOCK))
    packed = jnp.stack([jnp.concatenate([folded[g * per_row + j] for j in range(per_row)], axis=-1)
                        for g in range(n // per_row)])
    lane = lax.broadcasted_iota(jnp.int32, (SCAN_BLOCK, PACK_W), 1)
    eye = (lax.broadcasted_iota(jnp.int32, (SCAN_BLOCK, PACK_W), 0) == lane % SCAN_BLOCK).astype(F32)
    on_diag = ((lax.broadcasted_iota(jnp.int32, (PACK_W, PACK_W), 0) // SCAN_BLOCK)
               == (lax.broadcasted_iota(jnp.int32, (PACK_W, PACK_W), 1) // SCAN_BLOCK)).astype(BF16)

    def spread(y):
        return jnp.concatenate([y] * (PACK_W // SCAN_BLOCK), axis=1) * on_diag

    def mm(x, y):
        dg = lambda p, q: lax.dot_general(p, q, (_NN, ((0,), (0,))), preferred_element_type=F32)
        if INV_PASSES == 1:
            return dg(x.astype(BF16), spread(y.astype(BF16)))
        xh, xl = _split2(x)
        yh, yl = _split2(y)
        yh_d = spread(yh)
        return dg(xh, yh_d) + (dg(xh, spread(yl)) + dg(xl, yh_d))

    x = _series_inverse(packed, eye, SCAN_BLOCK, mm)
    blocks = [x[g][:, j * c:(j + 1) * c] for g in range(n // per_row) for j in range(per_row)]
    return jnp.stack([jnp.where(same_block, jnp.concatenate([blk] * (c // SCAN_BLOCK), axis=0), 0.0)
                      for blk in blocks])


def _unit_lower_inverse(a, c):
    row = lax.broadcasted_iota(jnp.int32, (c, c), 0)
    col = lax.broadcasted_iota(jnp.int32, (c, c), 1)
    eye = (row == col).astype(F32)
    same_block = (row // SCAN_BLOCK) == (col // SCAN_BLOCK)
    a_diag = jnp.where(same_block, a, 0.0)
    t_diag = _diag_block_inverse(a_diag, same_block, c)
    mm = functools.partial(_bmm, passes=INV_PASSES)
    b = mm(t_diag, a - a_diag)
    return mm(_series_inverse(b, eye, c // SCAN_BLOCK, mm), t_diag)


def _scan_body(valid, u_ref, sh_ref, mu_ref, w0_ref, w2_ref, a0_ref, a2_ref, g2_ref, kk_ref, ka_ref, hsum_ref,
               s0_ref, rk_ref, lnw_ref, lnb_ref, y_ref, so_ref, s_ref, prev_ref):
    t = pl.program_id(1)
    nb, c, _ = u_ref.shape

    @pl.when(t == 0)
    def _():
        s_ref[...] = s0_ref[...]
        prev_ref[...] = sh_ref[...]

    shifted = []
    for i in range(nb):
        u = u_ref[i]
        ext = jnp.concatenate([prev_ref[i], u], axis=0)
        prev_ref[i] = ext[c:]
        shifted.append(u + (pltpu.roll(ext, 1, 0)[_ROW_PAD:] - u) * mu_ref[...])
    xs = jnp.concatenate(shifted, axis=0)
    low = xs[:, 3 * GROUP_W:]
    w = -_softplus(-(w0_ref[...] + _dot(jnp.tanh(low), w2_ref[...]))) - 0.5
    a = jax.nn.sigmoid(a0_ref[...] + _dot(low, a2_ref[...]))
    k_raw = xs[:, GROUP_W:2 * GROUP_W]
    kk = k_raw * kk_ref[...]
    norm = jnp.sqrt(_dot_split(kk * kk, hsum_ref[...]))
    kap = kk / jnp.maximum(norm, 1e-12)
    steps = [xs[:, 0:GROUP_W],
             -jnp.exp(w),
             k_raw * (1.0 + (a - 1.0) * ka_ref[...]),
             xs[:, 2 * GROUP_W:3 * GROUP_W],
             kap,
             kap * a]
    if valid < c:
        live = lax.broadcasted_iota(jnp.int32, (nb * c, GROUP_W), 0) % c < valid
        steps = [jnp.where(live, x, 0.0) for x in steps]
    r_all, lw, k_all, v_all, kap_all, b_all = (x.reshape(nb, c, GROUP_W) for x in steps)
    gate = _dot(jax.nn.sigmoid(low), g2_ref[...]).reshape(nb, c, GROUP_W)

    row = lax.broadcasted_iota(jnp.int32, (c, c), 0)
    col = lax.broadcasted_iota(jnp.int32, (c, c), 1)
    incl = row >= col
    strict = row > col
    tri = incl.astype(BF16)

    def heads(x):
        return jnp.stack([x[i][:, h * HEAD_DIM:(h + 1) * HEAD_DIM]
                          for i in range(nb) for h in range(N_HEADS)])

    lw_hi, lw_lo = _split2(lw)
    lw_lo2 = (lw - lw_hi.astype(F32) - lw_lo.astype(F32)).astype(BF16)
    cum = jnp.stack([jnp.dot(tri, lw_hi[i], preferred_element_type=F32)
                     + (jnp.dot(tri, lw_lo[i], preferred_element_type=F32)
                        + jnp.dot(tri, lw_lo2[i], preferred_element_type=F32)) for i in range(nb)])
    cmid = cum[:, c // 2 - 1:c // 2, :]
    cend = cum[:, c - 1:c, :]
    e_bwd = jnp.exp(cmid - cum)
    e_end = jnp.exp(cend - cum)
    r_t = heads(r_all * jnp.exp(cum - cmid))
    kap_t = heads(kap_all * jnp.exp(cum - lw - cmid))
    k_t = heads(k_all * e_bwd)
    b_t = heads(b_all * e_bwd)
    r_0 = heads(r_all * jnp.exp(cum))
    kap_0 = heads(kap_all * jnp.exp(cum - lw))
    k_e = heads(k_all * e_end)
    b_e = heads(b_all * e_end)
    g_end = heads(jnp.exp(cend))
    v = heads(v_all)
    s = s_ref[...].reshape(nb * N_HEADS, HEAD_DIM, HEAD_DIM)

    stack = lambda top, bottom: jnp.concatenate([top, bottom], axis=1)
    with_b = _bmm(stack(kap_t, r_t), b_t, _NT)
    with_k = _bmm(stack(kap_t, r_t), k_t, _NT)
    a_kb = jnp.where(strict, with_b[:, :c], 0.0)
    a_rb = jnp.where(incl, with_b[:, c:], 0.0)
    a_kk = jnp.where(strict, with_k[:, :c], 0.0)
    a_rk = jnp.where(incl, with_k[:, c:], 0.0)
    t_inv = _unit_lower_inverse(a_kb, c)

    from_s = _bmm(stack(kap_0, r_0), s, _NT, STATE_PASSES)
    from_v = _bmm(stack(a_kk, a_rk), v)
    u = _bmm(t_inv, from_s[:, :c] + from_v[:, :c], passes=STATE_PASSES)
    y = from_s[:, c:] + from_v[:, c:] - _bmm(a_rb, u)
    s_new = s * g_end + _bmm(stack(v, u), stack(k_e, -b_e), _TN, STATE_PASSES)
    s_new = s_new.reshape(nb, N_HEADS, HEAD_DIM, HEAD_DIM)
    s_ref[...] = s_new
    so_ref[...] = s_new

    mean = jnp.mean(y, axis=-1, keepdims=True)
    yc = y - mean
    var = jnp.mean(yc * yc, axis=-1, keepdims=True)
    yn = yc * lax.rsqrt(var + GN_EPS)
    bonus = jnp.sum(heads(r_all * k_all * rk_ref[...]), axis=-1, keepdims=True) * v
    for i in range(nb):
        wide = lambda x: jnp.concatenate([x[i * N_HEADS + h] for h in range(N_HEADS)], axis=-1)
        y_ref[i] = ((wide(yn) * lnw_ref[...] + lnb_ref[...] + wide(bonus)) * gate[i]).astype(BF16)


def _rwkv_mixer(u, shift_state, s0, p):
    bsz, tlen, _ = u.shape
    c = SCAN_CHUNK
    nb = SCAN_BATCH
    u = _pad_time(u, c)
    sh = jnp.pad(shift_state[:, None, :], ((0, 0), (_ROW_PAD - 1, 0), (0, 0)))
    st = pl.BlockSpec((nb, N_HEADS, HEAD_DIM, HEAD_DIM), lambda b_, t: (b_, 0, 0, 0))
    vec = _full((1, GROUP_W))
    low = _full((RWKV_LOWRANK_W, GROUP_W))
    y, state = pl.pallas_call(
        functools.partial(_scan_body, min(tlen, c)),
        grid=(bsz // nb, u.shape[1] // c),
        in_specs=[pl.BlockSpec((nb, c, RWKV_IN), lambda b_, t: (b_, t, 0)),
                  pl.BlockSpec((nb, _ROW_PAD, RWKV_IN), lambda b_, t: (b_, 0, 0)),
                  _full((1, RWKV_IN)), vec, low, vec, low, low, vec, vec, _full((GROUP_W, GROUP_W)),
                  st, vec, vec, vec],
        out_specs=[pl.BlockSpec((nb, c, GROUP_W), lambda b_, t: (b_, t, 0)), st],
        out_shape=[jax.ShapeDtypeStruct((bsz, u.shape[1], GROUP_W), BF16),
                   jax.ShapeDtypeStruct((bsz, N_HEADS, HEAD_DIM, HEAD_DIM), F32)],
        scratch_shapes=[pltpu.VMEM((nb, N_HEADS, HEAD_DIM, HEAD_DIM), F32),
                        pltpu.VMEM((nb, _ROW_PAD, RWKV_IN), F32)],
        compiler_params=_cparams(("arbitrary", "arbitrary")),
        name="rwkv_mixer",
    )(u, sh, p["mu"], p["w0"], p["w2p"], p["a0"], p["a2p"], p["g2p"], p["k_k"], p["k_a"], p["hsum"],
      s0, p["r_k"], p["ln_w"], p["ln_b"])
    return y[:, :tlen], state


def _attn_body(past, q_ref, k_ref, v_ref, o_ref, kb_ref, vb_ref):
    qi = pl.program_id(1)
    tq = q_ref.shape[1]
    tk = ATTN_TILE
    q0 = past + qi * tq
    diag = q0 // tk
    scale = HEAD_DIM ** -0.5 * LOG2_E

    @pl.when(qi == 0)
    def _():
        kb_ref[...] = k_ref[0].astype(BF16)
        vb_ref[...] = v_ref[0].astype(BF16)

    heads = [slice(h * HEAD_DIM, (h + 1) * HEAD_DIM) for h in range(N_HEADS)]
    qs = [(q_ref[0, :, sl] * scale).astype(BF16) for sl in heads]
    later = (lax.broadcasted_iota(jnp.int32, (tk, tk), 0)
             > lax.broadcasted_iota(jnp.int32, (tk, tk), 1)).astype(BF16)
    every = range(N_HEADS)

    def tiles(js, carry, masked):
        accs, runs = carry
        units = [(t, h) for t in range(len(js)) for h in every]
        kt = [kb_ref[pl.ds(pl.multiple_of(j * tk, tk), tk), :] for j in js]
        vt = [vb_ref[pl.ds(pl.multiple_of(j * tk, tk), tk), :] for j in js]
        if masked:
            vis = [(j * tk + lax.broadcasted_iota(jnp.int32, (tq, tk), 1)
                    < q0 + lax.broadcasted_iota(jnp.int32, (tq, tk), 0)) for j in js]
        z = [_dot_nt(qs[h], kt[t][:, heads[h]]) for t, h in units]
        log_take = [jnp.minimum(x, 0.0) - jnp.log2(1.0 + jnp.exp2(-jnp.abs(x))) for x in z]
        log_keep = [a - b for a, b in zip(log_take, z)]
        if masked:
            log_keep = [jnp.where(vis[t], x, 0.0) for (t, _), x in zip(units, log_keep)]
        after = [_dot(x, later) for x in log_keep]
        p = [jnp.exp2(a + b) for a, b in zip(log_take, after)]
        if masked:
            p = [jnp.where(vis[t], x, 0.0) for (t, _), x in zip(units, p)]
        part = [jnp.dot(x.astype(BF16), vt[t][:, heads[h]], preferred_element_type=F32)
                for (t, h), x in zip(units, p)]
        accs, runs = list(accs), list(runs)
        for n, (_, h) in enumerate(units):
            accs[h] = accs[h] + jnp.exp2(runs[h]) * part[n]
            runs[h] = runs[h] + (after[n][:, 0:1] + log_keep[n][:, 0:1])
        return tuple(accs), tuple(runs)

    carry = (tuple(jnp.zeros((tq, HEAD_DIM), F32) for _ in every),
             tuple(jnp.zeros((tq, 1), F32) for _ in every))
    carry = tiles([diag], carry, True)
    assert ATTN_UNROLL == 4
    left_over = diag % ATTN_UNROLL
    carry = lax.cond(left_over % 2 == 1, lambda c: tiles([diag - 1], c, False), lambda c: c, carry)
    top = diag - left_over % 2 - 1
    carry = lax.cond(left_over >= 2, lambda c: tiles([top, top - 1], c, False), lambda c: c, carry)
    first = diag - left_over - 1
    accs, _ = lax.fori_loop(
        0, diag // ATTN_UNROLL,
        lambda i, c: tiles([first - ATTN_UNROLL * i - t for t in range(ATTN_UNROLL)], c, False), carry)
    o_ref[0] = jnp.concatenate(accs, axis=-1).astype(BF16)


def _sb_attn(q, k_all, v_all, past):
    bsz, tlen, _ = q.shape
    tq = min(ATTN_TILE, tlen)
    slen = k_all.shape[1]
    kv = pl.BlockSpec((1, slen, GROUP_W), lambda b, i: (b, 0, 0))
    return pl.pallas_call(
        functools.partial(_attn_body, past),
        grid=(bsz, tlen // tq),
        in_specs=[pl.BlockSpec((1, tq, GROUP_W), lambda b, i: (b, i, 0)), kv, kv],
        out_specs=pl.BlockSpec((1, tq, GROUP_W), lambda b, i: (b, i, 0)),
        out_shape=jax.ShapeDtypeStruct((bsz, tlen, GROUP_W), BF16),
        scratch_shapes=[pltpu.VMEM((slen, GROUP_W), BF16), pltpu.VMEM((slen, GROUP_W), BF16)],
        compiler_params=_cparams(("arbitrary", "arbitrary")),
        name="sb_attn",
    )(q, k_all, v_all)


def _pad_time(x, mult):
    pad = (-x.shape[1]) % mult
    return x if pad == 0 else jnp.pad(x, ((0, 0), (0, pad), (0, 0)))


def _prep_params(prm):
    out = []
    depth = prm["w_in"].shape[0]
    head_of = jnp.arange(GROUP_W) // HEAD_DIM
    hsum = (head_of[:, None] == head_of[None, :]).astype(BF16)
    row2 = lambda v: v.reshape(1, -1)
    for l in range(depth):
        w_bd = jax.scipy.linalg.block_diag(*[prm["pool_w"][l, i] for i in range(len(POOL_WINDOWS))])
        lowpad = lambda w, lo: jnp.zeros((RWKV_LOWRANK_W, GROUP_W), F32).at[lo:lo + w.shape[0]].set(w)
        low_at = [sum(RWKV_RANKS[:i]) for i in range(len(RWKV_RANKS))]
        out.append(dict(
            g=[row2(prm["norm_g"][l, i]) for i in range(6)],
            wg=[prm["ffn_w_gate"][l, i].astype(BF16) for i in range(2)],
            wu=[prm["ffn_w_up"][l, i].astype(BF16) for i in range(2)],
            wd=[prm["ffn_w_down"][l, i].astype(BF16) for i in range(2)],
            w_in=prm["w_in"][l].astype(BF16),
            w_out=prm["w_out"][l].astype(BF16),
            w_bd=w_bd.astype(BF16),
            pool_scale=row2(prm["pool_scale"][l]),
            conv_w=prm["conv_w"][l],
            mu=row2(prm["rwkv_mu"][l]),
            w0=row2(prm["rwkv_w0"][l]),
            w2p=lowpad(prm["rwkv_w2"][l], low_at[0]).astype(BF16),
            a0=row2(prm["rwkv_a0"][l]),
            a2p=lowpad(prm["rwkv_a2"][l], low_at[1]).astype(BF16),
            g2p=lowpad(prm["rwkv_g2"][l], low_at[2]).astype(BF16),
            k_k=row2(prm["rwkv_k_k"][l]),
            k_a=row2(prm["rwkv_k_a"][l]),
            r_k=row2(prm["rwkv_r_k"][l]),
            ln_w=row2(prm["rwkv_ln_w"][l]),
            ln_b=row2(prm["rwkv_ln_b"][l]),
            hsum=hsum,
        ))
    return out


def _trunk(x, k_past, v_past, wkv0, shift0, conv0, pool0, layers, past):
    bsz, tlen, _ = x.shape
    n = bsz * tlen
    x = x.reshape(n, D_MODEL)
    ks, vs, wkvs, shifts, convs, pools = [], [], [], [], [], []
    for l, p in enumerate(layers):
        x = _ffn(x, p["g"][0], p["g"][1], p["wg"][0], p["wu"][0], p["wd"][0])
        y_pool, u_rwkv, y_conv, q, k, v, pool_new, conv_new = _inproj(
            x, tlen, p["g"][2], p["w_in"], pool0[l], conv0[l], p["w_bd"], p["pool_scale"], p["conv_w"], past)
        seq = lambda a: a.reshape(bsz, tlen, -1)
        u_rwkv, q, k, v = map(seq, (u_rwkv, q, k, v))

        y_rwkv, wkv_new = _rwkv_mixer(u_rwkv, shift0[l], wkv0[l], p)

        if past:
            with_past = lambda old, new: _pad_time(
                jnp.concatenate([old[l].reshape(bsz, past, GROUP_W), new], axis=1), ATTN_TILE)
            y_sb = _sb_attn(q, with_past(k_past, k), with_past(v_past, v), past)
        else:
            y_sb = _sb_attn(q, k, v, past)

        flat = lambda a: a.reshape(n, GROUP_W)
        x = _ffn(x, p["g"][4], p["g"][5], p["wg"][1], p["wu"][1], p["wd"][1],
                 mix=(y_pool, flat(y_rwkv), y_conv, flat(y_sb), p["w_out"], p["g"][3]))

        ks.append(k.reshape(bsz, tlen, N_HEADS, HEAD_DIM))
        vs.append(v.reshape(bsz, tlen, N_HEADS, HEAD_DIM))
        wkvs.append(wkv_new)
        shifts.append(u_rwkv[:, -1])
        convs.append(conv_new)
        pools.append(pool_new[:, _POOL_PAD - POOL_STATE:])
    return (x.reshape(bsz, tlen, D_MODEL), jnp.stack(ks), jnp.stack(vs), jnp.stack(wkvs),
            jnp.stack(shifts), jnp.stack(convs), jnp.stack(pools))


def kernel(x_prompt, x_sample, cache_sb_k, cache_sb_v, state_wkv, state_shift, state_conv, state_pool,
           norm_g, ffn_w_gate, ffn_w_up, ffn_w_down, w_in, w_out, pool_w, pool_scale,
           rwkv_mu, rwkv_w0, rwkv_w2, rwkv_a0, rwkv_a2, rwkv_g2, rwkv_k_k, rwkv_k_a, rwkv_r_k,
           rwkv_ln_w, rwkv_ln_b, conv_w):
    layers = _prep_params(dict(
        norm_g=norm_g, ffn_w_gate=ffn_w_gate, ffn_w_up=ffn_w_up, ffn_w_down=ffn_w_down, w_in=w_in,
        w_out=w_out, pool_w=pool_w, pool_scale=pool_scale, rwkv_mu=rwkv_mu, rwkv_w0=rwkv_w0,
        rwkv_w2=rwkv_w2, rwkv_a0=rwkv_a0, rwkv_a2=rwkv_a2, rwkv_g2=rwkv_g2, rwkv_k_k=rwkv_k_k,
        rwkv_k_a=rwkv_k_a, rwkv_r_k=rwkv_r_k, rwkv_ln_w=rwkv_ln_w, rwkv_ln_b=rwkv_ln_b, conv_w=conv_w))
    depth = w_in.shape[0]
    bsz = x_prompt.shape[0]
    z_wkv = jnp.zeros((depth, bsz, N_HEADS, HEAD_DIM, HEAD_DIM), F32)
    z_shift = jnp.zeros((depth, bsz, RWKV_IN), F32)
    z_conv = jnp.zeros((depth, bsz, CONV_W - 1, GROUP_W), F32)
    z_pool = jnp.zeros((depth, bsz, POOL_STATE, GROUP_W), F32)
    prompt = _trunk(x_prompt, None, None, z_wkv, z_shift, z_conv, z_pool, layers, 0)
    sample = _trunk(x_sample, cache_sb_k, cache_sb_v, state_wkv, state_shift, state_conv, state_pool,
                    layers, cache_sb_k.shape[2])
    return (prompt[0], sample[0]) + prompt[1:] + sample[1:]
```

```python
import functools

import jax
import jax.numpy as jnp
from jax import lax
from jax.experimental import pallas as pl
from jax.experimental.pallas import tpu as pltpu

F32 = jnp.float32
BF16 = jnp.bfloat16

D_MODEL = 1024
GROUP_W = 256
HEAD_DIM = 64
N_HEADS = 4
POOL_WINDOWS = (2, 4, 8, 16)
POOL_STATE = 15
CONV_W = 3
RWKV_RANKS = (32, 32, 64)
RWKV_LOWRANK_W = sum(RWKV_RANKS)
RWKV_IN = 3 * GROUP_W + RWKV_LOWRANK_W
D_FF = 2816
RMS_EPS = 1e-6
GN_EPS = 64e-5
LOG2_E = 1.4426950408889634

FF_CHUNK = 256
TOKEN_TILE = 512
PROJ_TILE = 1024
SCAN_CHUNK = 64
SCAN_BLOCK = 16
PACK_W = 256
SCAN_BATCH = 8
INV_PASSES = 3
STATE_PASSES = 3
ATTN_TILE = 256
ATTN_UNROLL = 4
VMEM_LIMIT = 56 * 1024 * 1024


def _cparams(sem):
    return pltpu.CompilerParams(dimension_semantics=sem, vmem_limit_bytes=VMEM_LIMIT)


def _rms(x, g):
    return x * lax.rsqrt(jnp.mean(x * x, axis=-1, keepdims=True) + RMS_EPS) * g


def _dot(a, b):
    return jnp.dot(a.astype(BF16), b.astype(BF16), preferred_element_type=F32)


def _dot_nt(a, b):
    return lax.dot_general(a, b, (((1,), (1,)), ((), ())), preferred_element_type=F32)


def _split2(a):
    hi = a.astype(BF16)
    return hi, (a - hi.astype(F32)).astype(BF16)


def _dot_split(a, ones_b):
    hi, lo = _split2(a)
    return (jnp.dot(hi, ones_b, preferred_element_type=F32)
            + jnp.dot(lo, ones_b, preferred_element_type=F32))


def _full(shape):
    return pl.BlockSpec(shape, lambda *_: (0,) * len(shape))


def _resident(shape):
    return pl.BlockSpec(shape, lambda *_: (0,) * len(shape), pipeline_mode=pl.Buffered(1))


def _ffn_body(mixed, x_ref, *refs):
    if mixed:
        mix_refs, wo_ref, gmix_ref = refs[:4], refs[4], refs[5]
        refs = refs[6:]
        y = sum(jnp.dot(m_ref[...], wo_ref[i * GROUP_W:(i + 1) * GROUP_W, :], preferred_element_type=F32)
                for i, m_ref in enumerate(mix_refs))
        x = x_ref[...] + _rms(y, gmix_ref[...])
    else:
        x = x_ref[...]
    gpre_ref, gpost_ref, wg_ref, wu_ref, wd_ref, o_ref, a_ref = refs
    hb = _rms(x, gpre_ref[...]).astype(BF16)
    for j in range(D_FF // FF_CHUNK):
        sl = slice(j * FF_CHUNK, (j + 1) * FF_CHUNK)
        g = jnp.dot(hb, wg_ref[:, sl], preferred_element_type=F32)
        u = jnp.dot(hb, wu_ref[:, sl], preferred_element_type=F32)
        a_ref[:, sl] = (g * jax.nn.sigmoid(g) * u).astype(BF16)
    y = jnp.dot(a_ref[...], wd_ref[...], preferred_element_type=F32)
    o_ref[...] = x + 0.5 * _rms(y, gpost_ref[...])


def _ffn(x, g_pre, g_post, wg, wu, wd, mix=None):
    n = x.shape[0]
    tm = min(TOKEN_TILE, n)
    row = pl.BlockSpec((tm, D_MODEL), lambda i: (i, 0))
    grp = pl.BlockSpec((tm, GROUP_W), lambda i: (i, 0))
    mix_specs = [] if mix is None else [grp] * 4 + [_resident((D_MODEL, D_MODEL)), _full((1, D_MODEL))]
    return pl.pallas_call(
        functools.partial(_ffn_body, mix is not None),
        grid=(n // tm,),
        in_specs=[row] + mix_specs + [_full((1, D_MODEL)), _full((1, D_MODEL)), _resident((D_MODEL, D_FF)),
                                      _resident((D_MODEL, D_FF)), _resident((D_FF, D_MODEL))],
        out_specs=row,
        out_shape=jax.ShapeDtypeStruct((n, D_MODEL), F32),
        scratch_shapes=[pltpu.VMEM((tm, D_FF), BF16)],
        compiler_params=_cparams(("arbitrary",)),
        name="ffn",
    )(x, *([] if mix is None else mix), g_pre, g_post, wg, wu, wd)


_PROJ_WIDTHS = (GROUP_W, RWKV_IN, 3 * GROUP_W, GROUP_W, GROUP_W, GROUP_W)
_PROJ_EDGES = tuple(sum(_PROJ_WIDTHS[:i]) for i in range(len(_PROJ_WIDTHS) + 1))
_POOL_PAD = 16
_ROW_PAD = 8


def _inproj_body(past, tiles_per_seq, x_ref, g_ref, w_ref, pst_ref, cst_ref, wbd_ref, scale_ref, cw_ref,
                 yp_ref, ur_ref, yc_ref, q_ref, k_ref, v_ref, pso_ref, cso_ref, pprev_ref, cprev_ref):
    t = pl.program_id(0) % tiles_per_seq
    tm = x_ref.shape[0]

    @pl.when(t == 0)
    def _():
        pprev_ref[...] = pst_ref[0]
        cprev_ref[...] = cst_ref[0]

    hb = _rms(x_ref[...], g_ref[...]).astype(BF16)
    proj = lambda i: jnp.dot(hb, w_ref[:, _PROJ_EDGES[i]:_PROJ_EDGES[i + 1]], preferred_element_type=F32)
    ur_ref[...] = proj(1)
    q_ref[...] = proj(3)
    k_ref[...] = proj(4)
    v_ref[...] = proj(5)

    u = proj(0)
    ext = jnp.concatenate([pprev_ref[...], u], axis=0)
    s2 = ext + pltpu.roll(ext, 1, 0)
    s4 = s2 + pltpu.roll(s2, 2, 0)
    s8 = s4 + pltpu.roll(s4, 4, 0)
    s16 = s8 + pltpu.roll(s8, 8, 0)
    grp = lax.broadcasted_iota(jnp.int32, (tm, GROUP_W), 1) // (GROUP_W // len(POOL_WINDOWS))
    pos1 = past + t * tm + lax.broadcasted_iota(jnp.int32, (tm, GROUP_W), 0) + 1
    win = jnp.where(grp == 0, s2[_POOL_PAD:], jnp.where(grp == 1, s4[_POOL_PAD:],
                    jnp.where(grp == 2, s8[_POOL_PAD:], s16[_POOL_PAD:])))
    width = jnp.where(grp == 0, 2, jnp.where(grp == 1, 4, jnp.where(grp == 2, 8, 16)))
    cnt = jnp.minimum(width, pos1).astype(F32)
    d = win / cnt - u
    yp_ref[...] = (_dot(d, wbd_ref[...]) * scale_ref[...]).astype(BF16)
    pprev_ref[...] = ext[tm:]
    pso_ref[0] = ext[tm:]

    uc = proj(2)
    b = uc[:, 0:GROUP_W]
    z = uc[:, GROUP_W:2 * GROUP_W] * uc[:, 2 * GROUP_W:3 * GROUP_W]
    extz = jnp.concatenate([cprev_ref[...], z], axis=0)
    y = (pltpu.roll(extz, 2, 0)[_ROW_PAD:] * cw_ref[0:1, :]
         + pltpu.roll(extz, 1, 0)[_ROW_PAD:] * cw_ref[1:2, :]
         + z * cw_ref[2:3, :])
    yc_ref[...] = (b * y).astype(BF16)
    cprev_ref[...] = extz[tm:]
    cso_ref[0] = extz[tm:][_ROW_PAD - (CONV_W - 1):]


def _inproj(x, tlen, g, w_in, pool_state, conv_state, w_bd, scale, conv_w, past):
    n = x.shape[0]
    bsz = n // tlen
    tm = min(PROJ_TILE, tlen)
    tiles_per_seq = tlen // tm
    pst = jnp.pad(pool_state, ((0, 0), (_POOL_PAD - POOL_STATE, 0), (0, 0)))
    cst = jnp.pad(conv_state, ((0, 0), (_ROW_PAD - (CONV_W - 1), 0), (0, 0)))
    row = lambda w: pl.BlockSpec((tm, w), lambda i: (i, 0))
    per_seq = lambda r: pl.BlockSpec((1, r, GROUP_W), lambda i: (i // tiles_per_seq, 0, 0))
    flat = lambda w, dt: jax.ShapeDtypeStruct((n, w), dt)
    return pl.pallas_call(
        functools.partial(_inproj_body, past, tiles_per_seq),
        grid=(n // tm,),
        in_specs=[row(D_MODEL), _full((1, D_MODEL)), _resident((D_MODEL, _PROJ_EDGES[-1])),
                  per_seq(_POOL_PAD), per_seq(_ROW_PAD),
                  _full((GROUP_W, GROUP_W)), _full((1, GROUP_W)), _full((CONV_W, GROUP_W))],
        out_specs=[row(GROUP_W), row(RWKV_IN), row(GROUP_W), row(GROUP_W), row(GROUP_W), row(GROUP_W),
                   per_seq(_POOL_PAD), per_seq(CONV_W - 1)],
        out_shape=[flat(GROUP_W, BF16), flat(RWKV_IN, F32), flat(GROUP_W, BF16), flat(GROUP_W, F32),
                   flat(GROUP_W, F32), flat(GROUP_W, F32),
                   jax.ShapeDtypeStruct((bsz, _POOL_PAD, GROUP_W), F32),
                   jax.ShapeDtypeStruct((bsz, CONV_W - 1, GROUP_W), F32)],
        scratch_shapes=[pltpu.VMEM((_POOL_PAD, GROUP_W), F32), pltpu.VMEM((_ROW_PAD, GROUP_W), F32)],
        compiler_params=_cparams(("arbitrary",)),
        name="inproj",
    )(x, g, w_in, pst, cst, w_bd, scale, conv_w)


def _softplus(x):
    return jnp.maximum(x, 0.0) + jnp.log1p(jnp.exp(-jnp.abs(x)))


_NN = ((2,), (1,))
_NT = ((2,), (2,))
_TN = ((1,), (1,))


def _bmm(a, b, dims=_NN, passes=1):
    dg = lambda x, y: lax.dot_general(x, y, (dims, ((0,), (0,))), preferred_element_type=F32)
    if passes == 1:
        return dg(a.astype(BF16), b.astype(BF16))
    ah, al = _split2(a)
    bh, bl = _split2(b)
    if dims == _TN:
        return dg(ah, bh) + (dg(ah, bl) + dg(al, bh))
    m = a.shape[1]
    both = dg(jnp.concatenate([ah, al], axis=1), bh)
    return both[:, :m] + (both[:, m:] + dg(ah, bl))


def _series_inverse(m, eye, order, mm):
    x = eye - m
    p = m
    n = 2
    while n < order:
        p = mm(p, p)
        x = x + mm(x, p)
        n *= 2
    return x


def _diag_block_inverse(a_diag, same_block, c):
    n = a_diag.shape[0]
    per_row = PACK_W // c
    folded = sum(a_diag[:, i * SCAN_BLOCK:(i + 1) * SCAN_BLOCK, :] for i in range(c // SCAN_BLOCK))
    packed = jnp.stack([jnp.concatenate([folded[g * per_row + j] for j in range(per_row)], axis=-1)
                        for g in range(n // per_row)])
    lane = lax.broadcasted_iota(jnp.int32, (SCAN_BLOCK, PACK_W), 1)
    eye = (lax.broadcasted_iota(jnp.int32, (SCAN_BLOCK, PACK_W), 0) == lane % SCAN_BLOCK).astype(F32)
    on_diag = ((lax.broadcasted_iota(jnp.int32, (PACK_W, PACK_W), 0) // SCAN_BLOCK)
               == (lax.broadcasted_iota(jnp.int32, (PACK_W, PACK_W), 1) // SCAN_BLOCK)).astype(BF16)

    def spread(y):
        return jnp.concatenate([y] * (PACK_W // SCAN_BLOCK), axis=1) * on_diag

    def mm(x, y):
        dg = lambda p, q: lax.dot_general(p, q, (_NN, ((0,), (0,))), preferred_element_type=F32)
        if INV_PASSES == 1:
            return dg(x.astype(BF16), spread(y.astype(BF16)))
        xh, xl = _split2(x)
        yh, yl = _split2(y)
        yh_d = spread(yh)
        return dg(xh, yh_d) + (dg(xh, spread(yl)) + dg(xl, yh_d))

    x = _series_inverse(packed, eye, SCAN_BLOCK, mm)
    blocks = [x[g][:, j * c:(j + 1) * c] for g in range(n // per_row) for j in range(per_row)]
    return jnp.stack([jnp.where(same_block, jnp.concatenate([blk] * (c // SCAN_BLOCK), axis=0), 0.0)
                      for blk in blocks])


def _unit_lower_inverse(a, c):
    row = lax.broadcasted_iota(jnp.int32, (c, c), 0)
    col = lax.broadcasted_iota(jnp.int32, (c, c), 1)
    eye = (row == col).astype(F32)
    same_block = (row // SCAN_BLOCK) == (col // SCAN_BLOCK)
    a_diag = jnp.where(same_block, a, 0.0)
    t_diag = _diag_block_inverse(a_diag, same_block, c)
    mm = functools.partial(_bmm, passes=INV_PASSES)
    b = mm(t_diag, a - a_diag)
    return mm(_series_inverse(b, eye, c // SCAN_BLOCK, mm), t_diag)


def _scan_body(valid, u_ref, sh_ref, mu_ref, w0_ref, w2_ref, a0_ref, a2_ref, g2_ref, kk_ref, ka_ref, hsum_ref,
               s0_ref, rk_ref, lnw_ref, lnb_ref, y_ref, so_ref, s_ref, prev_ref):
    t = pl.program_id(1)
    nb, c, _ = u_ref.shape

    @pl.when(t == 0)
    def _():
        s_ref[...] = s0_ref[...]
        prev_ref[...] = sh_ref[...]

    shifted = []
    for i in range(nb):
        u = u_ref[i]
        ext = jnp.concatenate([prev_ref[i], u], axis=0)
        prev_ref[i] = ext[c:]
        shifted.append(u + (pltpu.roll(ext, 1, 0)[_ROW_PAD:] - u) * mu_ref[...])
    xs = jnp.concatenate(shifted, axis=0)
    low = xs[:, 3 * GROUP_W:]
    w = -_softplus(-(w0_ref[...] + _dot(jnp.tanh(low), w2_ref[...]))) - 0.5
    a = jax.nn.sigmoid(a0_ref[...] + _dot(low, a2_ref[...]))
    k_raw = xs[:, GROUP_W:2 * GROUP_W]
    kk = k_raw * kk_ref[...]
    norm = jnp.sqrt(_dot_split(kk * kk, hsum_ref[...]))
    kap = kk / jnp.maximum(norm, 1e-12)
    steps = [xs[:, 0:GROUP_W],
             -jnp.exp(w),
             k_raw * (1.0 + (a - 1.0) * ka_ref[...]),
             xs[:, 2 * GROUP_W:3 * GROUP_W],
             kap,
             kap * a]
    if valid < c:
        live = lax.broadcasted_iota(jnp.int32, (nb * c, GROUP_W), 0) % c < valid
        steps = [jnp.where(live, x, 0.0) for x in steps]
    r_all, lw, k_all, v_all, kap_all, b_all = (x.reshape(nb, c, GROUP_W) for x in steps)
    gate = _dot(jax.nn.sigmoid(low), g2_ref[...]).reshape(nb, c, GROUP_W)

    row = lax.broadcasted_iota(jnp.int32, (c, c), 0)
    col = lax.broadcasted_iota(jnp.int32, (c, c), 1)
    incl = row >= col
    strict = row > col
    tri = incl.astype(BF16)

    def heads(x):
        return jnp.stack([x[i][:, h * HEAD_DIM:(h + 1) * HEAD_DIM]
                          for i in range(nb) for h in range(N_HEADS)])

    lw_hi, lw_lo = _split2(lw)
    lw_lo2 = (lw - lw_hi.astype(F32) - lw_lo.astype(F32)).astype(BF16)
    cum = jnp.stack([jnp.dot(tri, lw_hi[i], preferred_element_type=F32)
                     + (jnp.dot(tri, lw_lo[i], preferred_element_type=F32)
                        + jnp.dot(tri, lw_lo2[i], preferred_element_type=F32)) for i in range(nb)])
    cmid = cum[:, c // 2 - 1:c // 2, :]
    cend = cum[:, c - 1:c, :]
    e_bwd = jnp.exp(cmid - cum)
    e_end = jnp.exp(cend - cum)
    r_t = heads(r_all * jnp.exp(cum - cmid))
    kap_t = heads(kap_all * jnp.exp(cum - lw - cmid))
    k_t = heads(k_all * e_bwd)
    b_t = heads(b_all * e_bwd)
    r_0 = heads(r_all * jnp.exp(cum))
    kap_0 = heads(kap_all * jnp.exp(cum - lw))
    k_e = heads(k_all * e_end)
    b_e = heads(b_all * e_end)
    g_end = heads(jnp.exp(cend))
    v = heads(v_all)
    s = s_ref[...].reshape(nb * N_HEADS, HEAD_DIM, HEAD_DIM)

    stack = lambda top, bottom: jnp.concatenate([top, bottom], axis=1)
    with_b = _bmm(stack(kap_t, r_t), b_t, _NT)
    with_k = _bmm(stack(kap_t, r_t), k_t, _NT)
    a_kb = jnp.where(strict, with_b[:, :c], 0.0)
    a_rb = jnp.where(incl, with_b[:, c:], 0.0)
    a_kk = jnp.where(strict, with_k[:, :c], 0.0)
    a_rk = jnp.where(incl, with_k[:, c:], 0.0)
    t_inv = _unit_lower_inverse(a_kb, c)

    from_s = _bmm(stack(kap_0, r_0), s, _NT, STATE_PASSES)
    from_v = _bmm(stack(a_kk, a_rk), v)
    u = _bmm(t_inv, from_s[:, :c] + from_v[:, :c], passes=STATE_PASSES)
    y = from_s[:, c:] + from_v[:, c:] - _bmm(a_rb, u)
    s_new = s * g_end + _bmm(stack(v, u), stack(k_e, -b_e), _TN, STATE_PASSES)
    s_new = s_new.reshape(nb, N_HEADS, HEAD_DIM, HEAD_DIM)
    s_ref[...] = s_new
    so_ref[...] = s_new

    mean = jnp.mean(y, axis=-1, keepdims=True)
    yc = y - mean
    var = jnp.mean(yc * yc, axis=-1, keepdims=True)
    yn = yc * lax.rsqrt(var + GN_EPS)
    bonus = jnp.sum(heads(r_all * k_all * rk_ref[...]), axis=-1, keepdims=True) * v
    for i in range(nb):
        wide = lambda x: jnp.concatenate([x[i * N_HEADS + h] for h in range(N_HEADS)], axis=-1)
        y_ref[i] = ((wide(yn) * lnw_ref[...] + lnb_ref[...] + wide(bonus)) * gate[i]).astype(BF16)


def _rwkv_mixer(u, shift_state, s0, p):
    bsz, tlen, _ = u.shape
    c = SCAN_CHUNK
    nb = SCAN_BATCH
    u = _pad_time(u, c)
    sh = jnp.pad(shift_state[:, None, :], ((0, 0), (_ROW_PAD - 1, 0), (0, 0)))
    st = pl.BlockSpec((nb, N_HEADS, HEAD_DIM, HEAD_DIM), lambda b_, t: (b_, 0, 0, 0))
    vec = _full((1, GROUP_W))
    low = _full((RWKV_LOWRANK_W, GROUP_W))
    y, state = pl.pallas_call(
        functools.partial(_scan_body, min(tlen, c)),
        grid=(bsz // nb, u.shape[1] // c),
        in_specs=[pl.BlockSpec((nb, c, RWKV_IN), lambda b_, t: (b_, t, 0)),
                  pl.BlockSpec((nb, _ROW_PAD, RWKV_IN), lambda b_, t: (b_, 0, 0)),
                  _full((1, RWKV_IN)), vec, low, vec, low, low, vec, vec, _full((GROUP_W, GROUP_W)),
                  st, vec, vec, vec],
        out_specs=[pl.BlockSpec((nb, c, GROUP_W), lambda b_, t: (b_, t, 0)), st],
        out_shape=[jax.ShapeDtypeStruct((bsz, u.shape[1], GROUP_W), BF16),
                   jax.ShapeDtypeStruct((bsz, N_HEADS, HEAD_DIM, HEAD_DIM), F32)],
        scratch_shapes=[pltpu.VMEM((nb, N_HEADS, HEAD_DIM, HEAD_DIM), F32),
                        pltpu.VMEM((nb, _ROW_PAD, RWKV_IN), F32)],
        compiler_params=_cparams(("arbitrary", "arbitrary")),
        name="rwkv_mixer",
    )(u, sh, p["mu"], p["w0"], p["w2p"], p["a0"], p["a2p"], p["g2p"], p["k_k"], p["k_a"], p["hsum"],
      s0, p["r_k"], p["ln_w"], p["ln_b"])
    return y[:, :tlen], state


def _attn_body(past, q_ref, k_ref, v_ref, o_ref, kb_ref, vb_ref):
    qi = pl.program_id(1)
    tq = q_ref.shape[1]
    tk = ATTN_TILE
    q0 = past + qi * tq
    diag = q0 // tk
    scale = HEAD_DIM ** -0.5 * LOG2_E

    @pl.when(qi == 0)
    def _():
        kb_ref[...] = k_ref[0].astype(BF16)
        vb_ref[...] = v_ref[0].astype(BF16)

    heads = [slice(h * HEAD_DIM, (h + 1) * HEAD_DIM) for h in range(N_HEADS)]
    qs = [(q_ref[0, :, sl] * scale).astype(BF16) for sl in heads]
    later = (lax.broadcasted_iota(jnp.int32, (tk, tk), 0)
             > lax.broadcasted_iota(jnp.int32, (tk, tk), 1)).astype(BF16)
    every = range(N_HEADS)

    def tiles(js, carry, masked):
        accs, runs = carry
        units = [(t, h) for t in range(len(js)) for h in every]
        kt = [kb_ref[pl.ds(pl.multiple_of(j * tk, tk), tk), :] for j in js]
        vt = [vb_ref[pl.ds(pl.multiple_of(j * tk, tk), tk), :] for j in js]
        if masked:
            vis = [(j * tk + lax.broadcasted_iota(jnp.int32, (tq, tk), 1)
                    < q0 + lax.broadcasted_iota(jnp.int32, (tq, tk), 0)) for j in js]
        z = [_dot_nt(qs[h], kt[t][:, heads[h]]) for t, h in units]
        log_take = [jnp.minimum(x, 0.0) - jnp.log2(1.0 + jnp.exp2(-jnp.abs(x))) for x in z]
        log_keep = [a - b for a, b in zip(log_take, z)]
        if masked:
            log_keep = [jnp.where(vis[t], x, 0.0) for (t, _), x in zip(units, log_keep)]
        after = [_dot(x, later) for x in log_keep]
        p = [jnp.exp2(a + b) for a, b in zip(log_take, after)]
        if masked:
            p = [jnp.where(vis[t], x, 0.0) for (t, _), x in zip(units, p)]
        part = [jnp.dot(x.astype(BF16), vt[t][:, heads[h]], preferred_element_type=F32)
                for (t, h), x in zip(units, p)]
        accs, runs = list(accs), list(runs)
        for n, (_, h) in enumerate(units):
            accs[h] = accs[h] + jnp.exp2(runs[h]) * part[n]
            runs[h] = runs[h] + (after[n][:, 0:1] + log_keep[n][:, 0:1])
        return tuple(accs), tuple(runs)

    carry = (tuple(jnp.zeros((tq, HEAD_DIM), F32) for _ in every),
             tuple(jnp.zeros((tq, 1), F32) for _ in every))
    carry = tiles([diag], carry, True)
    single = diag % ATTN_UNROLL
    carry = lax.fori_loop(0, single, lambda i, c: tiles([diag - 1 - i], c, False), carry)
    first = diag - single - 1
    accs, _ = lax.fori_loop(
        0, diag // ATTN_UNROLL,
        lambda i, c: tiles([first - ATTN_UNROLL * i - t for t in range(ATTN_UNROLL)], c, False), carry)
    o_ref[0] = jnp.concatenate(accs, axis=-1).astype(BF16)


def _sb_attn(q, k_all, v_all, past):
    bsz, tlen, _ = q.shape
    tq = min(ATTN_TILE, tlen)
    slen = k_all.shape[1]
    kv = pl.BlockSpec((1, slen, GROUP_W), lambda b, i: (b, 0, 0))
    return pl.pallas_call(
        functools.partial(_attn_body, past),
        grid=(bsz, tlen // tq),
        in_specs=[pl.BlockSpec((1, tq, GROUP_W), lambda b, i: (b, i, 0)), kv, kv],
        out_specs=pl.BlockSpec((1, tq, GROUP_W), lambda b, i: (b, i, 0)),
        out_shape=jax.ShapeDtypeStruct((bsz, tlen, GROUP_W), BF16),
        scratch_shapes=[pltpu.VMEM((slen, GROUP_W), BF16), pltpu.VMEM((slen, GROUP_W), BF16)],
        compiler_params=_cparams(("arbitrary", "arbitrary")),
        name="sb_attn",
    )(q, k_all, v_all)


def _pad_time(x, mult):
    pad = (-x.shape[1]) % mult
    return x if pad == 0 else jnp.pad(x, ((0, 0), (0, pad), (0, 0)))


def _prep_params(prm):
    out = []
    depth = prm["w_in"].shape[0]
    head_of = jnp.arange(GROUP_W) // HEAD_DIM
    hsum = (head_of[:, None] == head_of[None, :]).astype(BF16)
    row2 = lambda v: v.reshape(1, -1)
    for l in range(depth):
        w_bd = jax.scipy.linalg.block_diag(*[prm["pool_w"][l, i] for i in range(len(POOL_WINDOWS))])
        lowpad = lambda w, lo: jnp.zeros((RWKV_LOWRANK_W, GROUP_W), F32).at[lo:lo + w.shape[0]].set(w)
        low_at = [sum(RWKV_RANKS[:i]) for i in range(len(RWKV_RANKS))]
        out.append(dict(
            g=[row2(prm["norm_g"][l, i]) for i in range(6)],
            wg=[prm["ffn_w_gate"][l, i].astype(BF16) for i in range(2)],
            wu=[prm["ffn_w_up"][l, i].astype(BF16) for i in range(2)],
            wd=[prm["ffn_w_down"][l, i].astype(BF16) for i in range(2)],
            w_in=prm["w_in"][l].astype(BF16),
            w_out=prm["w_out"][l].astype(BF16),
            w_bd=w_bd.astype(BF16),
            pool_scale=row2(prm["pool_scale"][l]),
            conv_w=prm["conv_w"][l],
            mu=row2(prm["rwkv_mu"][l]),
            w0=row2(prm["rwkv_w0"][l]),
            w2p=lowpad(prm["rwkv_w2"][l], low_at[0]).astype(BF16),
            a0=row2(prm["rwkv_a0"][l]),
            a2p=lowpad(prm["rwkv_a2"][l], low_at[1]).astype(BF16),
            g2p=lowpad(prm["rwkv_g2"][l], low_at[2]).astype(BF16),
            k_k=row2(prm["rwkv_k_k"][l]),
            k_a=row2(prm["rwkv_k_a"][l]),
            r_k=row2(prm["rwkv_r_k"][l]),
            ln_w=row2(prm["rwkv_ln_w"][l]),
            ln_b=row2(prm["rwkv_ln_b"][l]),
            hsum=hsum,
        ))
    return out


def _trunk(x, k_past, v_past, wkv0, shift0, conv0, pool0, layers, past):
    bsz, tlen, _ = x.shape
    n = bsz * tlen
    x = x.reshape(n, D_MODEL)
    ks, vs, wkvs, shifts, convs, pools = [], [], [], [], [], []
    for l, p in enumerate(layers):
        x = _ffn(x, p["g"][0], p["g"][1], p["wg"][0], p["wu"][0], p["wd"][0])
        y_pool, u_rwkv, y_conv, q, k, v, pool_new, conv_new = _inproj(
            x, tlen, p["g"][2], p["w_in"], pool0[l], conv0[l], p["w_bd"], p["pool_scale"], p["conv_w"], past)
        seq = lambda a: a.reshape(bsz, tlen, -1)
        u_rwkv, q, k, v = map(seq, (u_rwkv, q, k, v))

        y_rwkv, wkv_new = _rwkv_mixer(u_rwkv, shift0[l], wkv0[l], p)

        if past:
            with_past = lambda old, new: _pad_time(
                jnp.concatenate([old[l].reshape(bsz, past, GROUP_W), new], axis=1), ATTN_TILE)
            y_sb = _sb_attn(q, with_past(k_past, k), with_past(v_past, v), past)
        else:
            y_sb = _sb_attn(q, k, v, past)

        flat = lambda a: a.reshape(n, GROUP_W)
        x = _ffn(x, p["g"][4], p["g"][5], p["wg"][1], p["wu"][1], p["wd"][1],
                 mix=(y_pool, flat(y_rwkv), y_conv, flat(y_sb), p["w_out"], p["g"][3]))

        ks.append(k.reshape(bsz, tlen, N_HEADS, HEAD_DIM))
        vs.append(v.reshape(bsz, tlen, N_HEADS, HEAD_DIM))
        wkvs.append(wkv_new)
        shifts.append(u_rwkv[:, -1])
        convs.append(conv_new)
        pools.append(pool_new[:, _POOL_PAD - POOL_STATE:])
    return (x.reshape(bsz, tlen, D_MODEL), jnp.stack(ks), jnp.stack(vs), jnp.stack(wkvs),
            jnp.stack(shifts), jnp.stack(convs), jnp.stack(pools))


def kernel(x_prompt, x_sample, cache_sb_k, cache_sb_v, state_wkv, state_shift, state_conv, state_pool,
           norm_g, ffn_w_gate, ffn_w_up, ffn_w_down, w_in, w_out, pool_w, pool_scale,
           rwkv_mu, rwkv_w0, rwkv_w2, rwkv_a0, rwkv_a2, rwkv_g2, rwkv_k_k, rwkv_k_a, rwkv_r_k,
           rwkv_ln_w, rwkv_ln_b, conv_w):
    layers = _prep_params(dict(
        norm_g=norm_g, ffn_w_gate=ffn_w_gate, ffn_w_up=ffn_w_up, ffn_w_down=ffn_w_down, w_in=w_in,
        w_out=w_out, pool_w=pool_w, pool_scale=pool_scale, rwkv_mu=rwkv_mu, rwkv_w0=rwkv_w0,
        rwkv_w2=rwkv_w2, rwkv_a0=rwkv_a0, rwkv_a2=rwkv_a2, rwkv_g2=rwkv_g2, rwkv_k_k=rwkv_k_k,
        rwkv_k_a=rwkv_k_a, rwkv_r_k=rwkv_r_k, rwkv_ln_w=rwkv_ln_w, rwkv_ln_b=rwkv_ln_b, conv_w=conv_w))
    depth = w_in.shape[0]
    bsz = x_prompt.shape[0]
    z_wkv = jnp.zeros((depth, bsz, N_HEADS, HEAD_DIM, HEAD_DIM), F32)
    z_shift = jnp.zeros((depth, bsz, RWKV_IN), F32)
    z_conv = jnp.zeros((depth, bsz, CONV_W - 1, GROUP_W), F32)
    z_pool = jnp.zeros((depth, bsz, POOL_STATE, GROUP_W), F32)
    prompt = _trunk(x_prompt, None, None, z_wkv, z_shift, z_conv, z_pool, layers, 0)
    sample = _trunk(x_sample, cache_sb_k, cache_sb_v, state_wkv, state_shift, state_conv, state_pool,
                    layers, cache_sb_k.shape[2])
    return (prompt[0], sample[0]) + prompt[1:] + sample[1:]
```
